```python
import math
import jax, jax.numpy as jnp
from jax import lax
import numpy as np

D_MODEL = 2048
BATCH = 2
SEQ = 8192
DEPTH = 2
DEC_BATCH = 2
DEC_SEQ = 16384
PAST_LEN = 128

N_META = 16
N_MIXERS = 2
N_POOL_GROUPS = 4
POOL_WINDOWS = (2, 4, 8, 16)
POOL_GROUP_DIM = D_MODEL // N_POOL_GROUPS
N_HEADS = 8
HEAD_DIM = D_MODEL // (2 * N_HEADS)
V_HEAD_DIM = 2 * HEAD_DIM
_FF_RAW = -(-8 * D_MODEL // 3)
D_FF = -(-_FF_RAW // 256) * 256
Q_BLOCK = 128
RMS_EPS = 1e-6
N_POOL_LAYERS = (DEPTH + 1) // 2
N_ATTN_LAYERS = DEPTH // 2

kernel_name = 'hybrid_pool_diffattn_encoder'


def rmsnorm(x, g):
    xf = x.astype(jnp.float32)
    y = xf * lax.rsqrt(jnp.mean(xf * xf, axis=-1, keepdims=True) + RMS_EPS) * g.astype(jnp.float32)
    return y.astype(x.dtype)


def centred_mean(xf, w):
    L = xf.shape[1]
    c = jnp.concatenate([jnp.zeros_like(xf[:, :1]), jnp.cumsum(xf, axis=1)], axis=1)
    t = jnp.arange(L)
    lo = jnp.clip(t - w // 2, 0, L)
    hi = jnp.clip(t + w // 2, 0, L)
    s = jnp.take(c, hi, axis=1) - jnp.take(c, lo, axis=1)
    cnt = (hi - lo).astype(jnp.float32)
    return s / cnt[None, :, None]


def pool_mixer(h, pool_w, pool_scale):
    B, L, _ = h.shape
    hf = h.astype(jnp.float32).reshape(B, L, N_POOL_GROUPS, POOL_GROUP_DIM)
    pooled = jnp.stack([centred_mean(hf[:, :, g], POOL_WINDOWS[g]) for g in range(N_POOL_GROUPS)], axis=2) - hf
    out = jnp.einsum('blgc,gcd->blgd', pooled.astype(h.dtype), pool_w).reshape(B, L, D_MODEL)
    return out * pool_scale


def diff_attention(h, w_qkv, lq1, lk1, lq2, lk2, subln_g, w_o, lambda_init):
    B, L, _ = h.shape
    Lp = -(-L // Q_BLOCK) * Q_BLOCK
    nb = Lp // Q_BLOCK
    qkv = h @ w_qkv
    q, k, v = jnp.split(qkv, 3, axis=-1)
    q = q.reshape(B, L, N_HEADS, 2, HEAD_DIM)
    k = k.reshape(B, L, N_HEADS, 2, HEAD_DIM)
    v = v.reshape(B, L, N_HEADS, V_HEAD_DIM)
    q = jnp.pad(q, ((0, 0), (0, Lp - L), (0, 0), (0, 0), (0, 0)))
    qb = q.reshape(B, nb, Q_BLOCK, N_HEADS, 2, HEAD_DIM).transpose(1, 0, 2, 3, 4, 5)
    qpos = jnp.arange(Lp).reshape(nb, Q_BLOCK)
    kpos = jnp.arange(L)
    slopes = 2.0 ** (-8.0 * (jnp.arange(N_HEADS, dtype=jnp.float32) + 1.0) / N_HEADS)
    lam = (jnp.exp(jnp.sum(lq1.astype(jnp.float32) * lk1.astype(jnp.float32)))
           - jnp.exp(jnp.sum(lq2.astype(jnp.float32) * lk2.astype(jnp.float32))) + lambda_init)
    scale = HEAD_DIM ** -0.5

    def block(args):
        qblk, qp = args
        s = jnp.einsum('bqhjd,bkhjd->bhjqk', qblk, k, preferred_element_type=jnp.float32) * scale
        dist = jnp.abs(qp[:, None] - kpos[None, :]).astype(jnp.float32)
        s = s - slopes[None, :, None, None, None] * dist[None, None, None]
        p = jax.nn.softmax(s, axis=-1)
        a = p[:, :, 0] - lam * p[:, :, 1]
        return jnp.einsum('bhqk,bkhe->bqhe', a.astype(v.dtype), v)

    o = lax.map(block, (qb, qpos))
    o = o.transpose(1, 0, 2, 3, 4).reshape(B, Lp, N_HEADS, V_HEAD_DIM)[:, :L]
    o = rmsnorm(o, subln_g) * (1.0 - lambda_init)
    return o.reshape(B, L, D_MODEL) @ w_o


def swiglu(h, w_gate, w_up, w_down):
    return (jax.nn.silu(h @ w_gate) * (h @ w_up)) @ w_down


def setup_inputs(seed: int = 0) -> dict:
    key = jax.random.key(seed)
    ks = jax.random.split(key, 24)
    f32 = jnp.float32
    D, G, GD = D_MODEL, N_POOL_GROUPS, POOL_GROUP_DIM
    nrm = lambda k, shape, s: jax.random.normal(k, shape, f32) * s
    return {
        'x_prompt': nrm(ks[0], (BATCH, SEQ, D), 1.0),
        'x_sample': nrm(ks[1], (DEC_BATCH, DEC_SEQ, D), 1.0),
        'meta_tokens': nrm(ks[2], (N_META, D), 1.0),
        'mixer_norm_g': 1.0 + nrm(ks[3], (DEPTH, D), 0.05),
        'pool_w': nrm(ks[4], (N_POOL_LAYERS, G, GD, GD), GD ** -0.5),
        'pool_scale': 1.0 + nrm(ks[5], (N_POOL_LAYERS, D), 0.1),
        'w_qkv': nrm(ks[6], (N_ATTN_LAYERS, D, 3 * D), D ** -0.5),
        'lambda_q1': nrm(ks[7], (N_ATTN_LAYERS, HEAD_DIM), 0.1),
        'lambda_k1': nrm(ks[8], (N_ATTN_LAYERS, HEAD_DIM), 0.1),
        'lambda_q2': nrm(ks[9], (N_ATTN_LAYERS, HEAD_DIM), 0.1),
        'lambda_k2': nrm(ks[10], (N_ATTN_LAYERS, HEAD_DIM), 0.1),
        'subln_g': 1.0 + nrm(ks[11], (N_ATTN_LAYERS, V_HEAD_DIM), 0.05),
        'w_o': nrm(ks[12], (N_ATTN_LAYERS, D, D), D ** -0.5),
        'ffn_norm_g': 1.0 + nrm(ks[13], (DEPTH, D), 0.05),
        'w_gate': nrm(ks[14], (DEPTH, D, D_FF), D ** -0.5),
        'w_up': nrm(ks[15], (DEPTH, D, D_FF), D ** -0.5),
        'w_down': nrm(ks[16], (DEPTH, D_FF, D), D_FF ** -0.5),
        'final_norm_g': 1.0 + nrm(ks[17], (D,), 0.05),
    }


def reference(x_prompt, x_sample, meta_tokens, mixer_norm_g, pool_w, pool_scale, w_qkv,
              lambda_q1, lambda_k1, lambda_q2, lambda_k2, subln_g, w_o, ffn_norm_g,
              w_gate, w_up, w_down, final_norm_g):
    def encode(x):
        B = x.shape[0]
        meta = jnp.broadcast_to(meta_tokens[None].astype(x.dtype), (B, N_META, D_MODEL))
        h = jnp.concatenate([meta, x], axis=1)
        for i in range(DEPTH):
            j = i // N_MIXERS
            hn = rmsnorm(h, mixer_norm_g[i])
            if i % N_MIXERS == 0:
                h = h + pool_mixer(hn, pool_w[j], pool_scale[j])
            else:
                lambda_init = 0.8 - 0.6 * math.exp(-0.3 * i)
                h = h + diff_attention(hn, w_qkv[j], lambda_q1[j], lambda_k1[j], lambda_q2[j],
                                       lambda_k2[j], subln_g[j], w_o[j], lambda_init)
            h = h + swiglu(rmsnorm(h, ffn_norm_g[i]), w_gate[i], w_up[i], w_down[i])
        h = rmsnorm(h, final_norm_g)
        return h[:, N_META:]

    y_prompt = encode(x_prompt)
    y_sample = encode(x_sample)
    return (y_prompt, y_sample)
```

```python
import functools
import math

import jax
import jax.numpy as jnp
from jax import lax
from jax.experimental import pallas as pl
from jax.experimental.pallas import tpu as pltpu

D_MODEL = 2048
N_META = 16
N_POOL_GROUPS = 4
POOL_WINDOWS = (2, 4, 8, 16)
POOL_GROUP_DIM = D_MODEL // N_POOL_GROUPS
N_HEADS = 8
HEAD_DIM = 128
V_HEAD_DIM = 2 * HEAD_DIM
D_FF = 5632
RMS_EPS = 1e-6

F32 = jnp.float32
BF16 = jnp.bfloat16

V7X_SUBLANES = 8
V7X_VMEM_BYTES = 64 * 1024 * 1024
V7X_VMEM_COMPILER_RESERVE = 8 * 1024 * 1024

SEQ_TILE = 512
TOKEN_TILE = 512
FF_TILE = 512
POOL_HALO = 8
SKIP_LOG_THRESHOLD = 90.0
MASKED_LOGIT = -1e30


def _vmem_limit(block_bytes):
    return min(V7X_VMEM_BYTES - V7X_VMEM_COMPILER_RESERVE,
               int(block_bytes) + V7X_VMEM_COMPILER_RESERVE)


def _rmsnorm(x, g):
    ms = jnp.mean(x * x, axis=-1, keepdims=True)
    return x * lax.rsqrt(ms + RMS_EPS) * g


def _pool_kernel(cur_ref, prev_ref, next_ref, g_ref, w_ref, scale_ref, out_ref, *, seq_len):
    i = pl.program_id(1)
    n_tiles = pl.num_programs(1)
    tile = cur_ref.shape[1]
    x = cur_ref[0]
    halo_lo = jnp.where(i > 0, prev_ref[0], 0.0)
    halo_hi = jnp.where(i < n_tiles - 1, next_ref[0], 0.0)
    ext = jnp.concatenate([halo_lo, x, halo_hi], axis=0)
    hn = _rmsnorm(ext, g_ref[...])
    n_ext = ext.shape[0]

    def shifted(a, d):
        return pltpu.roll(a, (-d) % n_ext, axis=0)

    pos = i * tile + lax.broadcasted_iota(jnp.int32, (tile, 1), 0)
    valid = pos < seq_len
    for gi, w in enumerate(POOL_WINDOWS):
        cols = slice(gi * POOL_GROUP_DIM, (gi + 1) * POOL_GROUP_DIM)
        e = hn[:, cols]
        s = shifted(e, -1) + e
        reach = 1
        while 2 * reach < w:
            s = shifted(s, -reach) + shifted(s, reach)
            reach *= 2
        win = s[POOL_HALO:POOL_HALO + tile]
        cnt = jnp.minimum(pos + w // 2, seq_len) - jnp.maximum(pos - w // 2, 0)
        cnt = jnp.maximum(cnt, 1).astype(F32)
        pooled = win / cnt - e[POOL_HALO:POOL_HALO + tile]
        y = jnp.dot(pooled.astype(BF16), w_ref[gi], preferred_element_type=F32)
        y = y * scale_ref[:, cols]
        out_ref[0, :, cols] = jnp.where(valid, x[:, cols] + y, 0.0)


def _pool_layer(h, g, w_bf16, scale, seq_len):
    bsz, lp, d = h.shape
    tile = SEQ_TILE
    nt = lp // tile
    hb = tile // POOL_HALO
    n_halo_blocks = lp // POOL_HALO
    blk = tile * d * 4
    return pl.pallas_call(
        functools.partial(_pool_kernel, seq_len=seq_len),
        out_shape=jax.ShapeDtypeStruct(h.shape, F32),
        grid=(bsz, nt),
        in_specs=[
            pl.BlockSpec((1, tile, d), lambda b, i: (b, i, 0)),
            pl.BlockSpec((1, POOL_HALO, d), lambda b, i: (b, jnp.maximum(i * hb - 1, 0), 0)),
            pl.BlockSpec((1, POOL_HALO, d),
                         lambda b, i: (b, jnp.minimum((i + 1) * hb, n_halo_blocks - 1), 0)),
            pl.BlockSpec((1, d), lambda b, i: (0, 0)),
            pl.BlockSpec((N_POOL_GROUPS, POOL_GROUP_DIM, POOL_GROUP_DIM), lambda b, i: (0, 0, 0)),
            pl.BlockSpec((1, d), lambda b, i: (0, 0)),
        ],
        out_specs=pl.BlockSpec((1, tile, d), lambda b, i: (b, i, 0)),
        compiler_params=pltpu.CompilerParams(
            dimension_semantics=("parallel", "arbitrary"),
            vmem_limit_bytes=_vmem_limit(4 * blk + 4 * w_bf16.size + 4 * blk)),
        name="pool_mixer",
    )(h, h, h, g.reshape(1, d), w_bf16, scale.reshape(1, d))


def _ffn_kernel(x_ref, g_ref, wg_ref, wu_ref, wd_ref, gf_ref, o_ref, hn_ref, *, final_norm):
    c = pl.program_id(1)

    @pl.when(c == 0)
    def _():
        x = x_ref[...]
        hn_ref[...] = _rmsnorm(x, g_ref[...]).astype(BF16)
        o_ref[...] = x

    hn = hn_ref[...]
    gate = jnp.dot(hn, wg_ref[...], preferred_element_type=F32)
    up = jnp.dot(hn, wu_ref[...], preferred_element_type=F32)
    act = (gate * jax.nn.sigmoid(gate) * up).astype(BF16)
    o_ref[...] += jnp.dot(act, wd_ref[...], preferred_element_type=F32)

    if final_norm:
        @pl.when(c == pl.num_programs(1) - 1)
        def _():
            o_ref[...] = _rmsnorm(o_ref[...], gf_ref[...])


def _ffn_layer(x, g, wg, wu, wd, gf, final_norm):
    n, d = x.shape
    tm, tf = TOKEN_TILE, FF_TILE
    dff = wg.shape[1]
    blocks = 2 * tm * d * 4 + 2 * tm * d * 4 + tm * d * 2 + 2 * 3 * d * tf * 2
    return pl.pallas_call(
        functools.partial(_ffn_kernel, final_norm=final_norm),
        out_shape=jax.ShapeDtypeStruct((n, d), F32),
        grid=(n // tm, dff // tf),
        in_specs=[
            pl.BlockSpec((tm, d), lambda i, c: (i, 0)),
            pl.BlockSpec((1, d), lambda i, c: (0, 0)),
            pl.BlockSpec((d, tf), lambda i, c: (0, c)),
            pl.BlockSpec((d, tf), lambda i, c: (0, c)),
            pl.BlockSpec((tf, d), lambda i, c: (c, 0)),
            pl.BlockSpec((1, d), lambda i, c: (0, 0)),
        ],
        out_specs=pl.BlockSpec((tm, d), lambda i, c: (i, 0)),
        scratch_shapes=[pltpu.VMEM((tm, d), BF16)],
        compiler_params=pltpu.CompilerParams(
            dimension_semantics=("parallel", "arbitrary"),
            vmem_limit_bytes=_vmem_limit(blocks + 3 * tm * tf * 4)),
        name="swiglu_ffn_final" if final_norm else "swiglu_ffn",
    )(x, g.reshape(1, d), wg, wu, wd, gf.reshape(1, d))


def _qkv_kernel(x_ref, g_ref, w_ref, q_ref, k_ref, v_ref):
    hn = _rmsnorm(x_ref[...], g_ref[...]).astype(BF16)
    for h in range(N_HEADS):
        for part, ref in ((0, q_ref), (1, k_ref)):
            c0 = part * D_MODEL + h * V_HEAD_DIM
            r = jnp.dot(hn, w_ref[:, c0:c0 + V_HEAD_DIM], preferred_element_type=F32)
            ref[0, h, 0] = r[:, :HEAD_DIM].astype(BF16)
            ref[0, h, 1] = r[:, HEAD_DIM:].astype(BF16)
        c0 = 2 * D_MODEL + h * V_HEAD_DIM
        v_ref[0, h] = jnp.dot(hn, w_ref[:, c0:c0 + V_HEAD_DIM],
                              preferred_element_type=F32).astype(BF16)


def _qkv_layer(x, g, w_bf16, bsz, lp):
    n, d = x.shape
    tm = TOKEN_TILE
    nt = lp // tm
    qk_shape = jax.ShapeDtypeStruct((bsz, N_HEADS, 2, lp, HEAD_DIM), BF16)
    v_shape = jax.ShapeDtypeStruct((bsz, N_HEADS, lp, V_HEAD_DIM), BF16)
    blocks = 2 * tm * d * 4 + w_bf16.size * 2 + 2 * 3 * tm * d * 2 + tm * d * 2
    return pl.pallas_call(
        _qkv_kernel,
        out_shape=(qk_shape, qk_shape, v_shape),
        grid=(n // tm,),
        in_specs=[
            pl.BlockSpec((tm, d), lambda i: (i, 0)),
            pl.BlockSpec((1, d), lambda i: (0, 0)),
            pl.BlockSpec((d, 3 * d), lambda i: (0, 0), pipeline_mode=pl.Buffered(1)),
        ],
        out_specs=(
            pl.BlockSpec((1, N_HEADS, 2, tm, HEAD_DIM), lambda i: (i // nt, 0, 0, i % nt, 0)),
            pl.BlockSpec((1, N_HEADS, 2, tm, HEAD_DIM), lambda i: (i // nt, 0, 0, i % nt, 0)),
            pl.BlockSpec((1, N_HEADS, tm, V_HEAD_DIM), lambda i: (i // nt, 0, i % nt, 0)),
        ),
        compiler_params=pltpu.CompilerParams(
            dimension_semantics=("parallel",),
            vmem_limit_bytes=_vmem_limit(blocks)),
        name="qkv_proj",
    )(x, g.reshape(1, d), w_bf16)


def _attn_kernel(lo_ref, hi_ref, slope_ref, q_ref, k_ref, v_ref, lq1_ref, lk1_ref, lq2_ref,
                 lk2_ref, subg_ref, o_ref, m_ref, l_ref, acc_ref, *, seq_len, lambda_init):
    b, h, qi = pl.program_id(0), pl.program_id(1), pl.program_id(2)
    n_q = pl.num_programs(2)
    tq = q_ref.shape[3]
    tk = tq
    flat = (b * N_HEADS + h) * n_q + qi
    lo, hi = lo_ref[flat], hi_ref[flat]
    slope = slope_ref[h]
    scale = HEAD_DIM ** -0.5

    m_ref[...] = jnp.full(m_ref.shape, MASKED_LOGIT, F32)
    l_ref[...] = jnp.zeros(l_ref.shape, F32)
    acc_ref[...] = jnp.zeros(acc_ref.shape, F32)

    row = qi * tq + lax.broadcasted_iota(jnp.int32, (tq, 1), 0)

    def step(j, carry):
        start = pl.multiple_of(j * tk, tk)
        col = start + lax.broadcasted_iota(jnp.int32, (1, tk), 1)
        dist = jnp.abs(row - col).astype(F32)
        bias = jnp.where(col < seq_len, -slope * dist, MASKED_LOGIT)
        vv = v_ref[0, 0, pl.ds(start, tk), :]
        for t in range(2):
            kk = k_ref[0, 0, t, pl.ds(start, tk), :]
            s = lax.dot_general(q_ref[0, 0, t], kk, (((1,), (1,)), ((), ())),
                                preferred_element_type=F32)
            s = s * scale + bias
            m_prev = m_ref[t]
            m_new = jnp.maximum(m_prev, jnp.max(s, axis=-1, keepdims=True))
            alpha = jnp.exp(m_prev - m_new)
            p = jnp.exp(s - m_new)
            l_ref[t] = alpha * l_ref[t] + jnp.sum(p, axis=-1, keepdims=True)
            acc_ref[t] = alpha * acc_ref[t] + jnp.dot(p.astype(BF16), vv,
                                                      preferred_element_type=F32)
            m_ref[t] = m_new
        return carry

    lax.fori_loop(lo, hi, step, 0)

    lam = (jnp.exp(jnp.sum(lq1_ref[...] * lk1_ref[...], axis=-1, keepdims=True))
           - jnp.exp(jnp.sum(lq2_ref[...] * lk2_ref[...], axis=-1, keepdims=True))
           + lambda_init)
    o = acc_ref[0] / l_ref[0] - lam * (acc_ref[1] / l_ref[1])
    o = _rmsnorm(o, subg_ref[...]) * (1.0 - lambda_init)
    o_ref[0] = o.astype(BF16)


def _key_tile_ranges(q, k, slopes, seq_len, tile):
    bsz, n_heads, _, lp, _ = q.shape
    n_t = lp // tile
    qn = jnp.sqrt(jnp.sum(jnp.square(q.astype(F32)), axis=-1))
    kn = jnp.sqrt(jnp.sum(jnp.square(k.astype(F32)), axis=-1))
    q_max = jnp.max(qn.reshape(bsz, n_heads, 2, n_t, tile), axis=-1)
    k_max = jnp.max(kn, axis=-1, keepdims=True)
    spread = 2.0 * HEAD_DIM ** -0.5 * jnp.max(q_max * k_max, axis=2) * 1.02 + 1.0
    reach = (SKIP_LOG_THRESHOLD + spread) / slopes[None, :, None]
    reach = jnp.ceil(jnp.minimum(reach, float(lp))).astype(jnp.int32)
    first_row = jnp.arange(n_t, dtype=jnp.int32)[None, None, :] * tile
    lo = jnp.maximum(first_row - reach, 0) // tile
    hi = jnp.minimum(first_row + tile - 1 + reach, seq_len - 1) // tile + 1
    return lo.reshape(-1), hi.reshape(-1)


def _attention_layer(q, k, v, lo, hi, slopes, lq1, lk1, lq2, lk2, subg, seq_len, lambda_init):
    bsz, n_heads, _, lp, dh = q.shape
    tq = SEQ_TILE
    n_q = lp // tq
    dv = v.shape[-1]
    kv_bytes = 2 * (2 * lp * dh * 2 + lp * dv * 2)
    blocks = kv_bytes + 2 * 2 * tq * dh * 2 + 2 * tq * dv * 2 + 2 * tq * dv * 4 + 4 * tq * 128 * 4
    vec = lambda b, h, i, *_: (0, 0)
    grid_spec = pltpu.PrefetchScalarGridSpec(
        num_scalar_prefetch=3,
        grid=(bsz, n_heads, n_q),
        in_specs=[
            pl.BlockSpec((1, 1, 2, tq, dh), lambda b, h, i, *_: (b, h, 0, i, 0)),
            pl.BlockSpec((1, 1, 2, lp, dh), lambda b, h, i, *_: (b, h, 0, 0, 0)),
            pl.BlockSpec((1, 1, lp, dv), lambda b, h, i, *_: (b, h, 0, 0)),
            pl.BlockSpec((1, dh), vec), pl.BlockSpec((1, dh), vec),
            pl.BlockSpec((1, dh), vec), pl.BlockSpec((1, dh), vec),
            pl.BlockSpec((1, dv), vec),
        ],
        out_specs=pl.BlockSpec((1, tq, dv), lambda b, h, i, *_: (b, i, h)),
        scratch_shapes=[
            pltpu.VMEM((2, tq, 1), F32),
            pltpu.VMEM((2, tq, 1), F32),
            pltpu.VMEM((2, tq, dv), F32),
        ],
    )
    return pl.pallas_call(
        functools.partial(_attn_kernel, seq_len=seq_len, lambda_init=lambda_init),
        out_shape=jax.ShapeDtypeStruct((bsz, lp, n_heads * dv), BF16),
        grid_spec=grid_spec,
        compiler_params=pltpu.CompilerParams(
            dimension_semantics=("parallel", "parallel", "arbitrary"),
            vmem_limit_bytes=_vmem_limit(blocks + 6 * tq * tq * 4)),
        name="diff_attention",
    )(lo, hi, slopes, q, k, v, lq1.reshape(1, dh), lk1.reshape(1, dh), lq2.reshape(1, dh),
      lk2.reshape(1, dh), subg.reshape(1, dv))


def _oproj_kernel(h_ref, o_ref, w_ref, out_ref):
    out_ref[...] = h_ref[...] + jnp.dot(o_ref[...], w_ref[...], preferred_element_type=F32)


def _oproj_layer(h, o, w_bf16):
    n, d = h.shape
    tm = TOKEN_TILE
    blocks = 4 * tm * d * 4 + 2 * tm * d * 2 + 2 * d * d * 2
    return pl.pallas_call(
        _oproj_kernel,
        out_shape=jax.ShapeDtypeStruct((n, d), F32),
        grid=(n // tm,),
        in_specs=[
            pl.BlockSpec((tm, d), lambda i: (i, 0)),
            pl.BlockSpec((tm, d), lambda i: (i, 0)),
            pl.BlockSpec((d, d), lambda i: (0, 0)),
        ],
        out_specs=pl.BlockSpec((tm, d), lambda i: (i, 0)),
        compiler_params=pltpu.CompilerParams(
            dimension_semantics=("parallel",),
            vmem_limit_bytes=_vmem_limit(blocks)),
        name="attn_out_proj",
    )(h, o, w_bf16)


def _encode(x, p):
    bsz, s, d = x.shape
    seq_len = s + N_META
    lp = -(-seq_len // SEQ_TILE) * SEQ_TILE
    meta = jnp.broadcast_to(p["meta_tokens"][None], (bsz, N_META, d))
    h = jnp.concatenate([meta, x, jnp.zeros((bsz, lp - seq_len, d), x.dtype)], axis=1)

    h = _pool_layer(h, p["mixer_norm_g"][0], p["pool_w"][0], p["pool_scale"][0], seq_len)
    h = h.reshape(bsz * lp, d)
    h = _ffn_layer(h, p["ffn_norm_g"][0], p["w_gate"][0], p["w_up"][0], p["w_down"][0],
                   p["final_norm_g"], final_norm=False)

    lambda_init = 0.8 - 0.6 * math.exp(-0.3 * 1)
    slopes = 2.0 ** (-8.0 * (jnp.arange(N_HEADS, dtype=F32) + 1.0) / N_HEADS)
    q, k, v = _qkv_layer(h, p["mixer_norm_g"][1], p["w_qkv"][0], bsz, lp)
    lo, hi = _key_tile_ranges(q, k, slopes, seq_len, SEQ_TILE)
    o = _attention_layer(q, k, v, lo, hi, slopes, p["lambda_q1"][0], p["lambda_k1"][0],
                         p["lambda_q2"][0], p["lambda_k2"][0], p["subln_g"][0], seq_len,
                         lambda_init)
    h = _oproj_layer(h, o.reshape(bsz * lp, d), p["w_o"][0])
    h = _ffn_layer(h, p["ffn_norm_g"][1], p["w_gate"][1], p["w_up"][1], p["w_down"][1],
                   p["final_norm_g"], final_norm=True)
    return h.reshape(bsz, lp, d)[:, N_META:seq_len]


def kernel(x_prompt, x_sample, meta_tokens, mixer_norm_g, pool_w, pool_scale, w_qkv, lambda_q1,
           lambda_k1, lambda_q2, lambda_k2, subln_g, w_o, ffn_norm_g, w_gate, w_up, w_down,
           final_norm_g):
    p = dict(
        meta_tokens=meta_tokens, mixer_norm_g=mixer_norm_g, pool_scale=pool_scale,
        lambda_q1=lambda_q1, lambda_k1=lambda_k1, lambda_q2=lambda_q2, lambda_k2=lambda_k2,
        subln_g=subln_g, ffn_norm_g=ffn_norm_g, final_norm_g=final_norm_g,
        pool_w=pool_w.astype(BF16), w_qkv=w_qkv.astype(BF16), w_o=w_o.astype(BF16),
        w_gate=w_gate.astype(BF16), w_up=w_up.astype(BF16), w_down=w_down.astype(BF16),
    )
    return (_encode(x_prompt, p), _encode(x_sample, p))
```

```python
import functools
import math

import jax
import jax.numpy as jnp
from jax import lax
from jax.experimental import pallas as pl
from jax.experimental.pallas import tpu as pltpu

D_MODEL = 2048
N_META = 16
N_POOL_GROUPS = 4
POOL_WINDOWS = (2, 4, 8, 16)
POOL_GROUP_DIM = D_MODEL // N_POOL_GROUPS
N_HEADS = 8
HEAD_DIM = 128
V_HEAD_DIM = 2 * HEAD_DIM
D_FF = 5632
RMS_EPS = 1e-6

F32 = jnp.float32
BF16 = jnp.bfloat16

V7X_SUBLANES = 8
V7X_VMEM_BYTES = 64 * 1024 * 1024
V7X_VMEM_COMPILER_RESERVE = 8 * 1024 * 1024

SEQ_TILE = 512
TOKEN_TILE = 512
FF_TILE = 512
POOL_HALO = 8
LOG2_E = math.log2(math.e)
QK_LOG2_SCALE = HEAD_DIM ** -0.5 * LOG2_E
SKIP_LOG2_THRESHOLD = 90.0 * LOG2_E
MASKED_LOGIT = -1e30
MIN_BOUNDED_ROW_SUM = 2.0 ** -60
BOUND_SLACK = 1.0 + 2.0 ** -6
LANES = 128


def _vmem_limit(block_bytes):
    return min(V7X_VMEM_BYTES - V7X_VMEM_COMPILER_RESERVE,
               int(block_bytes) + V7X_VMEM_COMPILER_RESERVE)


def _rmsnorm(x, g):
    ms = jnp.mean(x * x, axis=-1, keepdims=True)
    return x * lax.rsqrt(ms + RMS_EPS) * g


def _pool_kernel(cur_ref, prev_ref, next_ref, g_ref, w_ref, scale_ref, out_ref, *, seq_len):
    i = pl.program_id(1)
    n_tiles = pl.num_programs(1)
    tile = cur_ref.shape[1]
    x = cur_ref[0]
    halo_lo = jnp.where(i > 0, prev_ref[0], 0.0)
    halo_hi = jnp.where(i < n_tiles - 1, next_ref[0], 0.0)
    ext = jnp.concatenate([halo_lo, x, halo_hi], axis=0)
    hn = _rmsnorm(ext, g_ref[...])
    n_ext = ext.shape[0]

    def shifted(a, d):
        return pltpu.roll(a, (-d) % n_ext, axis=0)

    pos = i * tile + lax.broadcasted_iota(jnp.int32, (tile, 1), 0)
    valid = pos < seq_len
    for gi, w in enumerate(POOL_WINDOWS):
        cols = slice(gi * POOL_GROUP_DIM, (gi + 1) * POOL_GROUP_DIM)
        e = hn[:, cols]
        s = shifted(e, -1) + e
        reach = 1
        while 2 * reach < w:
            s = shifted(s, -reach) + shifted(s, reach)
            reach *= 2
        win = s[POOL_HALO:POOL_HALO + tile]
        cnt = jnp.minimum(pos + w // 2, seq_len) - jnp.maximum(pos - w // 2, 0)
        cnt = jnp.maximum(cnt, 1).astype(F32)
        pooled = win / cnt - e[POOL_HALO:POOL_HALO + tile]
        y = jnp.dot(pooled.astype(BF16), w_ref[gi], preferred_element_type=F32)
        y = y * scale_ref[:, cols]
        out_ref[0, :, cols] = jnp.where(valid, x[:, cols] + y, 0.0)


def _pool_layer(h, g, w_bf16, scale, seq_len):
    bsz, lp, d = h.shape
    tile = SEQ_TILE
    nt = lp // tile
    hb = tile // POOL_HALO
    n_halo_blocks = lp // POOL_HALO
    blk = tile * d * 4
    return pl.pallas_call(
        functools.partial(_pool_kernel, seq_len=seq_len),
        out_shape=jax.ShapeDtypeStruct(h.shape, F32),
        grid=(bsz, nt),
        in_specs=[
            pl.BlockSpec((1, tile, d), lambda b, i: (b, i, 0)),
            pl.BlockSpec((1, POOL_HALO, d), lambda b, i: (b, jnp.maximum(i * hb - 1, 0), 0)),
            pl.BlockSpec((1, POOL_HALO, d),
                         lambda b, i: (b, jnp.minimum((i + 1) * hb, n_halo_blocks - 1), 0)),
            pl.BlockSpec((1, d), lambda b, i: (0, 0)),
            pl.BlockSpec((N_POOL_GROUPS, POOL_GROUP_DIM, POOL_GROUP_DIM), lambda b, i: (0, 0, 0)),
            pl.BlockSpec((1, d), lambda b, i: (0, 0)),
        ],
        out_specs=pl.BlockSpec((1, tile, d), lambda b, i: (b, i, 0)),
        compiler_params=pltpu.CompilerParams(
            dimension_semantics=("parallel", "arbitrary"),
            vmem_limit_bytes=_vmem_limit(4 * blk + 4 * w_bf16.size + 4 * blk)),
        name="pool_mixer",
    )(h, h, h, g.reshape(1, d), w_bf16, scale.reshape(1, d))


def _ffn_kernel(x_ref, g_ref, wg_ref, wu_ref, wd_ref, gf_ref, o_ref, hn_ref, *, final_norm):
    c = pl.program_id(1)

    @pl.when(c == 0)
    def _():
        x = x_ref[...]
        hn_ref[...] = _rmsnorm(x, g_ref[...]).astype(BF16)
        o_ref[...] = x

    hn = hn_ref[...]
    gate = jnp.dot(hn, wg_ref[...], preferred_element_type=F32)
    up = jnp.dot(hn, wu_ref[...], preferred_element_type=F32)
    act = (gate * jax.nn.sigmoid(gate) * up).astype(BF16)
    o_ref[...] += jnp.dot(act, wd_ref[...], preferred_element_type=F32)

    if final_norm:
        @pl.when(c == pl.num_programs(1) - 1)
        def _():
            o_ref[...] = _rmsnorm(o_ref[...], gf_ref[...])


def _ffn_layer(x, g, wg, wu, wd, gf, final_norm):
    n, d = x.shape
    tm, tf = TOKEN_TILE, FF_TILE
    dff = wg.shape[1]
    blocks = 2 * tm * d * 4 + 2 * tm * d * 4 + tm * d * 2 + 2 * 3 * d * tf * 2
    return pl.pallas_call(
        functools.partial(_ffn_kernel, final_norm=final_norm),
        out_shape=jax.ShapeDtypeStruct((n, d), F32),
        grid=(n // tm, dff // tf),
        in_specs=[
            pl.BlockSpec((tm, d), lambda i, c: (i, 0)),
            pl.BlockSpec((1, d), lambda i, c: (0, 0)),
            pl.BlockSpec((d, tf), lambda i, c: (0, c)),
            pl.BlockSpec((d, tf), lambda i, c: (0, c)),
            pl.BlockSpec((tf, d), lambda i, c: (c, 0)),
            pl.BlockSpec((1, d), lambda i, c: (0, 0)),
        ],
        out_specs=pl.BlockSpec((tm, d), lambda i, c: (i, 0)),
        scratch_shapes=[pltpu.VMEM((tm, d), BF16)],
        compiler_params=pltpu.CompilerParams(
            dimension_semantics=("parallel", "arbitrary"),
            vmem_limit_bytes=_vmem_limit(blocks + 3 * tm * tf * 4)),
        name="swiglu_ffn_final" if final_norm else "swiglu_ffn",
    )(x, g.reshape(1, d), wg, wu, wd, gf.reshape(1, d))


def _qkv_kernel(x_ref, g_ref, w_ref, q_ref, k_ref, v_ref):
    hn = _rmsnorm(x_ref[...], g_ref[...]).astype(BF16)
    for h in range(N_HEADS):
        for part, ref in ((0, q_ref), (1, k_ref)):
            c0 = part * D_MODEL + h * V_HEAD_DIM
            r = jnp.dot(hn, w_ref[:, c0:c0 + V_HEAD_DIM], preferred_element_type=F32)
            if part == 0:
                r = r * QK_LOG2_SCALE
            ref[0, h, 0] = r[:, :HEAD_DIM].astype(BF16)
            ref[0, h, 1] = r[:, HEAD_DIM:].astype(BF16)
        c0 = 2 * D_MODEL + h * V_HEAD_DIM
        v_ref[0, h] = jnp.dot(hn, w_ref[:, c0:c0 + V_HEAD_DIM],
                              preferred_element_type=F32).astype(BF16)


def _qkv_layer(x, g, w_bf16, bsz, lp):
    n, d = x.shape
    tm = TOKEN_TILE
    nt = lp // tm
    qk_shape = jax.ShapeDtypeStruct((bsz, N_HEADS, 2, lp, HEAD_DIM), BF16)
    v_shape = jax.ShapeDtypeStruct((bsz, N_HEADS, lp, V_HEAD_DIM), BF16)
    blocks = 2 * tm * d * 4 + w_bf16.size * 2 + 2 * 3 * tm * d * 2 + tm * d * 2
    return pl.pallas_call(
        _qkv_kernel,
        out_shape=(qk_shape, qk_shape, v_shape),
        grid=(n // tm,),
        in_specs=[
            pl.BlockSpec((tm, d), lambda i: (i, 0)),
            pl.BlockSpec((1, d), lambda i: (0, 0)),
            pl.BlockSpec((d, 3 * d), lambda i: (0, 0), pipeline_mode=pl.Buffered(1)),
        ],
        out_specs=(
            pl.BlockSpec((1, N_HEADS, 2, tm, HEAD_DIM), lambda i: (i // nt, 0, 0, i % nt, 0)),
            pl.BlockSpec((1, N_HEADS, 2, tm, HEAD_DIM), lambda i: (i // nt, 0, 0, i % nt, 0)),
            pl.BlockSpec((1, N_HEADS, tm, V_HEAD_DIM), lambda i: (i // nt, 0, i % nt, 0)),
        ),
        compiler_params=pltpu.CompilerParams(
            dimension_semantics=("parallel",),
            vmem_limit_bytes=_vmem_limit(blocks)),
        name="qkv_proj",
    )(x, g.reshape(1, d), w_bf16)


N_BIAS_CLASSES = 5
_NT_DIMS = (((1,), (1,)), ((), ()))


def _attn_kernel(lo_ref, hi_ref, slope_ref, kmax_ref, q_ref, k_ref, v_ref, lq1_ref, lk1_ref,
                 lq2_ref, lk2_ref, subg_ref, o_ref, bias_ref, bound_ref, l_ref, acc_ref, m_ref,
                 ls_ref, *, seq_len, lambda_init):
    b, h, qi = pl.program_id(0), pl.program_id(1), pl.program_id(2)
    n_q = pl.num_programs(2)
    last = n_q - 1
    t = q_ref.shape[3]
    n_chunks = t // LANES
    flat = (b * N_HEADS + h) * n_q + qi
    lo, hi = lo_ref[flat], hi_ref[flat]
    sigma = slope_ref[h] * LOG2_E

    @pl.when(qi == 0)
    def _():
        r = lax.broadcasted_iota(jnp.int32, (t, t), 0)
        c = lax.broadcasted_iota(jnp.int32, (t, t), 1)
        d = (r - c).astype(F32)
        left = -sigma * d
        diag = -sigma * jnp.abs(d)
        col_ok = (last * t + c) < seq_len
        bias_ref[0] = left
        bias_ref[1] = diag
        bias_ref[2] = -left
        bias_ref[3] = jnp.where(col_ok, diag, MASKED_LOGIT)
        bias_ref[4] = jnp.where(col_ok, -left, MASKED_LOGIT)

    for m in range(2):
        qf = q_ref[0, 0, m].astype(F32)
        norm = jnp.sqrt(jnp.sum(qf * qf, axis=-1, keepdims=True))
        kmax = kmax_ref[(b * N_HEADS + h) * 2 + m]
        bound_ref[m] = jnp.broadcast_to(norm * (kmax * BOUND_SLACK), (t, LANES))
    l_ref[...] = jnp.zeros(l_ref.shape, F32)
    acc_ref[...] = jnp.zeros(acc_ref.shape, F32)

    def tile_class(j):
        side = jnp.where(j < qi, 0, jnp.where(j == qi, 1, 2))
        return side + jnp.where(jnp.logical_and(j == last, j >= qi), 2, 0)

    def tile_offset(j):
        gap = jnp.full((1, LANES), jnp.abs(qi - j) * t, jnp.int32).astype(F32)
        return gap * (-sigma)

    def bounded_step(j, carry):
        start = pl.multiple_of(j * t, t)
        cls = tile_class(j)
        off = tile_offset(j)
        vv = v_ref[0, 0, pl.ds(start, t), :]
        for m in range(2):
            kk = k_ref[0, 0, m, pl.ds(start, t), :]
            s = lax.dot_general(q_ref[0, 0, m], kk, _NT_DIMS, preferred_element_type=F32)
            shift = off - bound_ref[m]
            row_sum = None
            ps = []
            for cc in range(n_chunks):
                sl = slice(cc * LANES, (cc + 1) * LANES)
                p = jnp.exp2(s[:, sl] + bias_ref[cls, :, sl] + shift)
                row_sum = p if row_sum is None else row_sum + p
                ps.append(p.astype(BF16))
            l_ref[m] += row_sum
            acc_ref[m] += jnp.dot(jnp.concatenate(ps, axis=1), vv, preferred_element_type=F32)
        return carry

    lax.fori_loop(lo, hi, bounded_step, 0)

    row = qi * t + lax.broadcasted_iota(jnp.int32, (t, 1), 0)
    valid = row < seq_len
    sums = jnp.minimum(jnp.sum(l_ref[0], axis=-1, keepdims=True),
                       jnp.sum(l_ref[1], axis=-1, keepdims=True))
    smallest = jnp.min(jnp.where(valid, sums, 1.0))

    @pl.when(smallest < MIN_BOUNDED_ROW_SUM)
    def _():
        m_ref[...] = jnp.full(m_ref.shape, MASKED_LOGIT, F32)
        ls_ref[...] = jnp.zeros(ls_ref.shape, F32)
        acc_ref[...] = jnp.zeros(acc_ref.shape, F32)

        def online_step(j, carry):
            start = pl.multiple_of(j * t, t)
            cls = tile_class(j)
            off = tile_offset(j)[:, :1]
            vv = v_ref[0, 0, pl.ds(start, t), :]
            for m in range(2):
                kk = k_ref[0, 0, m, pl.ds(start, t), :]
                s = lax.dot_general(q_ref[0, 0, m], kk, _NT_DIMS, preferred_element_type=F32)
                s = s + bias_ref[cls] + off
                m_prev = m_ref[m]
                m_new = jnp.maximum(m_prev, jnp.max(s, axis=-1, keepdims=True))
                alpha = jnp.exp2(m_prev - m_new)
                p = jnp.exp2(s - m_new)
                ls_ref[m] = alpha * ls_ref[m] + jnp.sum(p, axis=-1, keepdims=True)
                acc_ref[m] = alpha * acc_ref[m] + jnp.dot(p.astype(BF16), vv,
                                                          preferred_element_type=F32)
                m_ref[m] = m_new
            return carry

        lax.fori_loop(lo, hi, online_step, 0)
        lane0 = lax.broadcasted_iota(jnp.int32, (t, LANES), 1) == 0
        for m in range(2):
            l_ref[m] = jnp.where(lane0, ls_ref[m], 0.0)

    lam = (jnp.exp(jnp.sum(lq1_ref[...] * lk1_ref[...], axis=-1, keepdims=True))
           - jnp.exp(jnp.sum(lq2_ref[...] * lk2_ref[...], axis=-1, keepdims=True))
           + lambda_init)
    l0 = jnp.sum(l_ref[0], axis=-1, keepdims=True)
    l1 = jnp.sum(l_ref[1], axis=-1, keepdims=True)
    o = acc_ref[0] / l0 - lam * (acc_ref[1] / l1)
    o = jnp.where(valid, o, 0.0)
    o = _rmsnorm(o, subg_ref[...]) * (1.0 - lambda_init)
    o_ref[0] = o.astype(BF16)


def _key_tile_ranges(q, k, slopes, seq_len, tile):
    bsz, n_heads, _, lp, _ = q.shape
    n_t = lp // tile
    qn = jnp.sqrt(jnp.sum(jnp.square(q.astype(F32)), axis=-1))
    kn = jnp.sqrt(jnp.sum(jnp.square(k.astype(F32)), axis=-1))
    q_max = jnp.max(qn.reshape(bsz, n_heads, 2, n_t, tile), axis=-1)
    k_max = jnp.max(kn, axis=-1, keepdims=True)
    spread = 2.0 * jnp.max(q_max * k_max, axis=2) * BOUND_SLACK + 1.0
    reach = (SKIP_LOG2_THRESHOLD + spread) / (slopes * LOG2_E)[None, :, None]
    reach = jnp.ceil(jnp.minimum(reach, float(lp))).astype(jnp.int32)
    first_row = jnp.arange(n_t, dtype=jnp.int32)[None, None, :] * tile
    lo = jnp.maximum(first_row - reach, 0) // tile
    hi = jnp.minimum(first_row + tile - 1 + reach, seq_len - 1) // tile + 1
    return lo.reshape(-1), hi.reshape(-1), k_max.reshape(-1)


def _attention_layer(q, k, v, lo, hi, slopes, kmax, lq1, lk1, lq2, lk2, subg, seq_len,
                     lambda_init):
    bsz, n_heads, _, lp, dh = q.shape
    tq = SEQ_TILE
    n_q = lp // tq
    dv = v.shape[-1]
    kv_bytes = 2 * (2 * lp * dh * 2 + lp * dv * 2)
    scratch = (N_BIAS_CLASSES * tq * tq + 4 * tq * LANES + 2 * tq * dv + 4 * tq * LANES) * 4
    blocks = kv_bytes + 2 * 2 * tq * dh * 2 + 2 * tq * dv * 2 + scratch
    vec = lambda b, h, i, *_: (0, 0)
    grid_spec = pltpu.PrefetchScalarGridSpec(
        num_scalar_prefetch=4,
        grid=(bsz, n_heads, n_q),
        in_specs=[
            pl.BlockSpec((1, 1, 2, tq, dh), lambda b, h, i, *_: (b, h, 0, i, 0)),
            pl.BlockSpec((1, 1, 2, lp, dh), lambda b, h, i, *_: (b, h, 0, 0, 0)),
            pl.BlockSpec((1, 1, lp, dv), lambda b, h, i, *_: (b, h, 0, 0)),
            pl.BlockSpec((1, dh), vec), pl.BlockSpec((1, dh), vec),
            pl.BlockSpec((1, dh), vec), pl.BlockSpec((1, dh), vec),
            pl.BlockSpec((1, dv), vec),
        ],
        out_specs=pl.BlockSpec((1, tq, dv), lambda b, h, i, *_: (b, i, h)),
        scratch_shapes=[
            pltpu.VMEM((N_BIAS_CLASSES, tq, tq), F32),
            pltpu.VMEM((2, tq, LANES), F32),
            pltpu.VMEM((2, tq, LANES), F32),
            pltpu.VMEM((2, tq, dv), F32),
            pltpu.VMEM((2, tq, 1), F32),
            pltpu.VMEM((2, tq, 1), F32),
        ],
    )
    return pl.pallas_call(
        functools.partial(_attn_kernel, seq_len=seq_len, lambda_init=lambda_init),
        out_shape=jax.ShapeDtypeStruct((bsz, lp, n_heads * dv), BF16),
        grid_spec=grid_spec,
        compiler_params=pltpu.CompilerParams(
            dimension_semantics=("arbitrary", "arbitrary", "arbitrary"),
            vmem_limit_bytes=_vmem_limit(blocks + 6 * tq * tq * 4)),
        name="diff_attention",
    )(lo, hi, slopes, kmax, q, k, v, lq1.reshape(1, dh), lk1.reshape(1, dh),
      lq2.reshape(1, dh), lk2.reshape(1, dh), subg.reshape(1, dv))


def _oproj_kernel(h_ref, o_ref, w_ref, out_ref):
    out_ref[...] = h_ref[...] + jnp.dot(o_ref[...], w_ref[...], preferred_element_type=F32)


def _oproj_layer(h, o, w_bf16):
    n, d = h.shape
    tm = TOKEN_TILE
    blocks = 4 * tm * d * 4 + 2 * tm * d * 2 + 2 * d * d * 2
    return pl.pallas_call(
        _oproj_kernel,
        out_shape=jax.ShapeDtypeStruct((n, d), F32),
        grid=(n // tm,),
        in_specs=[
            pl.BlockSpec((tm, d), lambda i: (i, 0)),
            pl.BlockSpec((tm, d), lambda i: (i, 0)),
            pl.BlockSpec((d, d), lambda i: (0, 0)),
        ],
        out_specs=pl.BlockSpec((tm, d), lambda i: (i, 0)),
        compiler_params=pltpu.CompilerParams(
            dimension_semantics=("parallel",),
            vmem_limit_bytes=_vmem_limit(blocks)),
        name="attn_out_proj",
    )(h, o, w_bf16)


def _encode(x, p):
    bsz, s, d = x.shape
    seq_len = s + N_META
    lp = -(-seq_len // SEQ_TILE) * SEQ_TILE
    meta = jnp.broadcast_to(p["meta_tokens"][None], (bsz, N_META, d))
    h = jnp.concatenate([meta, x, jnp.zeros((bsz, lp - seq_len, d), x.dtype)], axis=1)

    h = _pool_layer(h, p["mixer_norm_g"][0], p["pool_w"][0], p["pool_scale"][0], seq_len)
    h = h.reshape(bsz * lp, d)
    h = _ffn_layer(h, p["ffn_norm_g"][0], p["w_gate"][0], p["w_up"][0], p["w_down"][0],
                   p["final_norm_g"], final_norm=False)

    lambda_init = 0.8 - 0.6 * math.exp(-0.3 * 1)
    slopes = 2.0 ** (-8.0 * (jnp.arange(N_HEADS, dtype=F32) + 1.0) / N_HEADS)
    q, k, v = _qkv_layer(h, p["mixer_norm_g"][1], p["w_qkv"][0], bsz, lp)
    lo, hi, kmax = _key_tile_ranges(q, k, slopes, seq_len, SEQ_TILE)
    o = _attention_layer(q, k, v, lo, hi, slopes, kmax, p["lambda_q1"][0], p["lambda_k1"][0],
                         p["lambda_q2"][0], p["lambda_k2"][0], p["subln_g"][0], seq_len,
                         lambda_init)
    h = _oproj_layer(h, o.reshape(bsz * lp, d), p["w_o"][0])
    h = _ffn_layer(h, p["ffn_norm_g"][1], p["w_gate"][1], p["w_up"][1], p["w_down"][1],
                   p["final_norm_g"], final_norm=True)
    return h.reshape(bsz, lp, d)[:, N_META:seq_len]


def kernel(x_prompt, x_sample, meta_tokens, mixer_norm_g, pool_w, pool_scale, w_qkv, lambda_q1,
           lambda_k1, lambda_q2, lambda_k2, subln_g, w_o, ffn_norm_g, w_gate, w_up, w_down,
           final_norm_g):
    p = dict(
        meta_tokens=meta_tokens, mixer_norm_g=mixer_norm_g, pool_scale=pool_scale,
        lambda_q1=lambda_q1, lambda_k1=lambda_k1, lambda_q2=lambda_q2, lambda_k2=lambda_k2,
        subln_g=subln_g, ffn_norm_g=ffn_norm_g, final_norm_g=final_norm_g,
        pool_w=pool_w.astype(BF16), w_qkv=w_qkv.astype(BF16), w_o=w_o.astype(BF16),
        w_gate=w_gate.astype(BF16), w_up=w_up.astype(BF16), w_down=w_down.astype(BF16),
    )
    return (_encode(x_prompt, p), _encode(x_sample, p))
```

```python
import functools
import math

import jax
import jax.numpy as jnp
from jax import lax
from jax.experimental import pallas as pl
from jax.experimental.pallas import tpu as pltpu

D_MODEL = 2048
N_META = 16
N_POOL_GROUPS = 4
POOL_WINDOWS = (2, 4, 8, 16)
POOL_GROUP_DIM = D_MODEL // N_POOL_GROUPS
N_HEADS = 8
HEAD_DIM = 128
V_HEAD_DIM = 2 * HEAD_DIM
D_FF = 5632
RMS_EPS = 1e-6

F32 = jnp.float32
BF16 = jnp.bfloat16

V7X_SUBLANES = 8
V7X_VMEM_BYTES = 64 * 1024 * 1024
V7X_VMEM_COMPILER_RESERVE = 8 * 1024 * 1024

SEQ_TILE = 512
TOKEN_TILE = 512
FF_TILE = 512
POOL_HALO = 8
LOG2_E = math.log2(math.e)
QK_LOG2_SCALE = HEAD_DIM ** -0.5 * LOG2_E
SKIP_LOG2_THRESHOLD = 90.0 * LOG2_E
MASKED_LOGIT = -1e30
MIN_BOUNDED_ROW_SUM = 2.0 ** -60
BOUND_SLACK = 1.0 + 2.0 ** -6
LANES = 128


def _vmem_limit(block_bytes):
    return min(V7X_VMEM_BYTES - V7X_VMEM_COMPILER_RESERVE,
               int(block_bytes) + V7X_VMEM_COMPILER_RESERVE)


def _rmsnorm(x, g):
    ms = jnp.mean(x * x, axis=-1, keepdims=True)
    return x * lax.rsqrt(ms + RMS_EPS) * g


def _pool_kernel(cur_ref, prev_ref, next_ref, g_ref, w_ref, scale_ref, out_ref, *, seq_len):
    i = pl.program_id(1)
    n_tiles = pl.num_programs(1)
    tile = cur_ref.shape[1]
    x = cur_ref[0]
    halo_lo = jnp.where(i > 0, prev_ref[0], 0.0)
    halo_hi = jnp.where(i < n_tiles - 1, next_ref[0], 0.0)
    ext = jnp.concatenate([halo_lo, x, halo_hi], axis=0)
    hn = _rmsnorm(ext, g_ref[...])
    n_ext = ext.shape[0]

    def shifted(a, d):
        return pltpu.roll(a, (-d) % n_ext, axis=0)

    pos = i * tile + lax.broadcasted_iota(jnp.int32, (tile, 1), 0)
    valid = pos < seq_len
    for gi, w in enumerate(POOL_WINDOWS):
        cols = slice(gi * POOL_GROUP_DIM, (gi + 1) * POOL_GROUP_DIM)
        e = hn[:, cols]
        s = shifted(e, -1) + e
        reach = 1
        while 2 * reach < w:
            s = shifted(s, -reach) + shifted(s, reach)
            reach *= 2
        win = s[POOL_HALO:POOL_HALO + tile]
        cnt = jnp.minimum(pos + w // 2, seq_len) - jnp.maximum(pos - w // 2, 0)
        cnt = jnp.maximum(cnt, 1).astype(F32)
        pooled = win / cnt - e[POOL_HALO:POOL_HALO + tile]
        y = jnp.dot(pooled.astype(BF16), w_ref[gi], preferred_element_type=F32)
        y = y * scale_ref[:, cols]
        out_ref[0, :, cols] = jnp.where(valid, x[:, cols] + y, 0.0)


def _pool_layer(h, g, w_bf16, scale, seq_len):
    bsz, lp, d = h.shape
    tile = SEQ_TILE
    nt = lp // tile
    hb = tile // POOL_HALO
    n_halo_blocks = lp // POOL_HALO
    blk = tile * d * 4
    return pl.pallas_call(
        functools.partial(_pool_kernel, seq_len=seq_len),
        out_shape=jax.ShapeDtypeStruct(h.shape, F32),
        grid=(bsz, nt),
        in_specs=[
            pl.BlockSpec((1, tile, d), lambda b, i: (b, i, 0)),
            pl.BlockSpec((1, POOL_HALO, d), lambda b, i: (b, jnp.maximum(i * hb - 1, 0), 0)),
            pl.BlockSpec((1, POOL_HALO, d),
                         lambda b, i: (b, jnp.minimum((i + 1) * hb, n_halo_blocks - 1), 0)),
            pl.BlockSpec((1, d), lambda b, i: (0, 0)),
            pl.BlockSpec((N_POOL_GROUPS, POOL_GROUP_DIM, POOL_GROUP_DIM), lambda b, i: (0, 0, 0)),
            pl.BlockSpec((1, d), lambda b, i: (0, 0)),
        ],
        out_specs=pl.BlockSpec((1, tile, d), lambda b, i: (b, i, 0)),
        compiler_params=pltpu.CompilerParams(
            dimension_semantics=("parallel", "arbitrary"),
            vmem_limit_bytes=_vmem_limit(4 * blk + 4 * w_bf16.size + 4 * blk)),
        name="pool_mixer",
    )(h, h, h, g.reshape(1, d), w_bf16, scale.reshape(1, d))


def _ffn_kernel(x_ref, g_ref, wg_ref, wu_ref, wd_ref, gf_ref, o_ref, hn_ref, *, final_norm):
    c = pl.program_id(1)

    @pl.when(c == 0)
    def _():
        x = x_ref[...]
        hn_ref[...] = _rmsnorm(x, g_ref[...]).astype(BF16)
        o_ref[...] = x

    hn = hn_ref[...]
    gate = jnp.dot(hn, wg_ref[...], preferred_element_type=F32)
    up = jnp.dot(hn, wu_ref[...], preferred_element_type=F32)
    act = (gate * jax.nn.sigmoid(gate) * up).astype(BF16)
    o_ref[...] += jnp.dot(act, wd_ref[...], preferred_element_type=F32)

    if final_norm:
        @pl.when(c == pl.num_programs(1) - 1)
        def _():
            o_ref[...] = _rmsnorm(o_ref[...], gf_ref[...])


def _ffn_layer(x, g, wg, wu, wd, gf, final_norm):
    n, d = x.shape
    tm, tf = TOKEN_TILE, FF_TILE
    dff = wg.shape[1]
    blocks = 2 * tm * d * 4 + 2 * tm * d * 4 + tm * d * 2 + 2 * 3 * d * tf * 2
    return pl.pallas_call(
        functools.partial(_ffn_kernel, final_norm=final_norm),
        out_shape=jax.ShapeDtypeStruct((n, d), F32),
        grid=(n // tm, dff // tf),
        in_specs=[
            pl.BlockSpec((tm, d), lambda i, c: (i, 0)),
            pl.BlockSpec((1, d), lambda i, c: (0, 0)),
            pl.BlockSpec((d, tf), lambda i, c: (0, c)),
            pl.BlockSpec((d, tf), lambda i, c: (0, c)),
            pl.BlockSpec((tf, d), lambda i, c: (c, 0)),
            pl.BlockSpec((1, d), lambda i, c: (0, 0)),
        ],
        out_specs=pl.BlockSpec((tm, d), lambda i, c: (i, 0)),
        scratch_shapes=[pltpu.VMEM((tm, d), BF16)],
        compiler_params=pltpu.CompilerParams(
            dimension_semantics=("parallel", "arbitrary"),
            vmem_limit_bytes=_vmem_limit(blocks + 3 * tm * tf * 4)),
        name="swiglu_ffn_final" if final_norm else "swiglu_ffn",
    )(x, g.reshape(1, d), wg, wu, wd, gf.reshape(1, d))


def _qkv_kernel(x_ref, g_ref, w_ref, q_ref, k_ref, v_ref):
    hn = _rmsnorm(x_ref[...], g_ref[...]).astype(BF16)
    for h in range(N_HEADS):
        for part, ref in ((0, q_ref), (1, k_ref)):
            c0 = part * D_MODEL + h * V_HEAD_DIM
            r = jnp.dot(hn, w_ref[:, c0:c0 + V_HEAD_DIM], preferred_element_type=F32)
            if part == 0:
                r = r * QK_LOG2_SCALE
            ref[0, h, 0] = r[:, :HEAD_DIM].astype(BF16)
            ref[0, h, 1] = r[:, HEAD_DIM:].astype(BF16)
        c0 = 2 * D_MODEL + h * V_HEAD_DIM
        v_ref[0, h] = jnp.dot(hn, w_ref[:, c0:c0 + V_HEAD_DIM],
                              preferred_element_type=F32).astype(BF16)


def _qkv_layer(x, g, w_bf16, bsz, lp):
    n, d = x.shape
    tm = TOKEN_TILE
    nt = lp // tm
    qk_shape = jax.ShapeDtypeStruct((bsz, N_HEADS, 2, lp, HEAD_DIM), BF16)
    v_shape = jax.ShapeDtypeStruct((bsz, N_HEADS, lp, V_HEAD_DIM), BF16)
    blocks = 2 * tm * d * 4 + w_bf16.size * 2 + 2 * 3 * tm * d * 2 + tm * d * 2
    return pl.pallas_call(
        _qkv_kernel,
        out_shape=(qk_shape, qk_shape, v_shape),
        grid=(n // tm,),
        in_specs=[
            pl.BlockSpec((tm, d), lambda i: (i, 0)),
            pl.BlockSpec((1, d), lambda i: (0, 0)),
            pl.BlockSpec((d, 3 * d), lambda i: (0, 0), pipeline_mode=pl.Buffered(1)),
        ],
        out_specs=(
            pl.BlockSpec((1, N_HEADS, 2, tm, HEAD_DIM), lambda i: (i // nt, 0, 0, i % nt, 0)),
            pl.BlockSpec((1, N_HEADS, 2, tm, HEAD_DIM), lambda i: (i // nt, 0, 0, i % nt, 0)),
            pl.BlockSpec((1, N_HEADS, tm, V_HEAD_DIM), lambda i: (i // nt, 0, i % nt, 0)),
        ),
        compiler_params=pltpu.CompilerParams(
            dimension_semantics=("parallel",),
            vmem_limit_bytes=_vmem_limit(blocks)),
        name="qkv_proj",
    )(x, g.reshape(1, d), w_bf16)


N_BIAS_CLASSES = 5
_NT_DIMS = (((1,), (1,)), ((), ()))


def _attn_kernel(slope_ref, q_ref, k_ref, v_ref, lq1_ref, lk1_ref, lq2_ref, lk2_ref, subg_ref,
                 o_ref, bias_ref, kmax_ref, bound_ref, l_ref, acc_ref, m_ref, ls_ref, *,
                 seq_len, lambda_init):
    h, qi = pl.program_id(1), pl.program_id(2)
    n_q = pl.num_programs(2)
    last = n_q - 1
    t = q_ref.shape[3]
    n_chunks = t // LANES
    sigma = slope_ref[h] * LOG2_E

    @pl.when(qi == 0)
    def _():
        r = lax.broadcasted_iota(jnp.int32, (t, t), 0)
        c = lax.broadcasted_iota(jnp.int32, (t, t), 1)
        d = (r - c).astype(F32)
        left = -sigma * d
        diag = -sigma * jnp.abs(d)
        col_ok = (last * t + c) < seq_len
        bias_ref[0] = left
        bias_ref[1] = diag
        bias_ref[2] = -left
        bias_ref[3] = jnp.where(col_ok, diag, MASKED_LOGIT)
        bias_ref[4] = jnp.where(col_ok, -left, MASKED_LOGIT)
        for m in range(2):
            def widest(j, best, m=m):
                kk = k_ref[0, 0, m, pl.ds(pl.multiple_of(j * t, t), t), :].astype(F32)
                return jnp.maximum(best, jnp.sum(kk * kk, axis=-1, keepdims=True))
            best = lax.fori_loop(0, n_q, widest, jnp.zeros((t, 1), F32))
            kmax_ref[m] = jnp.broadcast_to(jnp.sqrt(jnp.max(best, axis=0, keepdims=True)),
                                           (1, LANES))

    products = []
    for m in range(2):
        qf = q_ref[0, 0, m].astype(F32)
        norm = jnp.sqrt(jnp.sum(qf * qf, axis=-1, keepdims=True))
        kmax = kmax_ref[m]
        bound_ref[m] = norm * (kmax * BOUND_SLACK)
        products.append(jnp.max(norm, axis=0, keepdims=True) * kmax)
    l_ref[...] = jnp.zeros(l_ref.shape, F32)
    acc_ref[...] = jnp.zeros(acc_ref.shape, F32)

    spread = 2.0 * BOUND_SLACK * jnp.maximum(products[0], products[1]) + 1.0
    reach = (SKIP_LOG2_THRESHOLD + spread) / sigma
    reach = jnp.ceil(jnp.minimum(reach, float(k_ref.shape[3]))) + 1.0
    first_row = jnp.full((1, LANES), qi * t, jnp.int32).astype(F32)
    lo_tile = jnp.floor(jnp.maximum(first_row - reach, 0.0) * (1.0 / t))
    hi_tile = jnp.floor(jnp.minimum(first_row + (t - 1.0) + reach, seq_len - 1.0) * (1.0 / t))
    lo = jnp.max(lo_tile).astype(jnp.int32)
    hi = jnp.max(hi_tile).astype(jnp.int32) + 1

    def tile_class(j):
        side = jnp.where(j < qi, 0, jnp.where(j == qi, 1, 2))
        return side + jnp.where(jnp.logical_and(j == last, j >= qi), 2, 0)

    def tile_offset(j):
        gap = jnp.full((1, LANES), jnp.abs(qi - j) * t, jnp.int32).astype(F32)
        return gap * (-sigma)

    def bounded_block(j0, n_tiles):
        start = pl.multiple_of(j0 * t, t)
        width = n_tiles * t
        classes = [tile_class(j0 + u) for u in range(n_tiles)]
        offsets = [tile_offset(j0 + u) for u in range(n_tiles)]
        vv = v_ref[0, 0, pl.ds(start, width), :]
        for m in range(2):
            kk = k_ref[0, 0, m, pl.ds(start, width), :]
            s = lax.dot_general(q_ref[0, 0, m], kk, _NT_DIMS, preferred_element_type=F32)
            row_sum = None
            ps = []
            for u in range(n_tiles):
                shift = offsets[u] - bound_ref[m]
                for cc in range(n_chunks):
                    sl = slice(cc * LANES, (cc + 1) * LANES)
                    su = s[:, u * t + cc * LANES:u * t + (cc + 1) * LANES]
                    p = jnp.exp2(su + bias_ref[classes[u], :, sl] + shift)
                    row_sum = p if row_sum is None else row_sum + p
                    ps.append(p.astype(BF16))
            l_ref[m] += row_sum
            acc_ref[m] += jnp.dot(jnp.concatenate(ps, axis=1), vv, preferred_element_type=F32)

    n_steps = hi - lo
    n_wide = lax.shift_right_logical(n_steps, 2)

    def wide_body(i, carry):
        bounded_block(lo + 4 * i, 4)
        return carry

    lax.fori_loop(0, n_wide, wide_body, 0)
    rest = lo + 4 * n_wide

    @pl.when(jnp.bitwise_and(n_steps, 2) != 0)
    def _():
        bounded_block(rest, 2)

    @pl.when(jnp.bitwise_and(n_steps, 1) != 0)
    def _():
        bounded_block(hi - 1, 1)

    row = qi * t + lax.broadcasted_iota(jnp.int32, (t, 1), 0)
    valid = row < seq_len
    for m in range(2):
        ls_ref[m] = jnp.sum(l_ref[m], axis=-1, keepdims=True)
    smallest = jnp.min(jnp.where(valid, jnp.minimum(ls_ref[0], ls_ref[1]), 1.0))

    @pl.when(smallest < MIN_BOUNDED_ROW_SUM)
    def _():
        m_ref[...] = jnp.full(m_ref.shape, MASKED_LOGIT, F32)
        ls_ref[...] = jnp.zeros(ls_ref.shape, F32)
        acc_ref[...] = jnp.zeros(acc_ref.shape, F32)

        def online_step(j, carry):
            start = pl.multiple_of(j * t, t)
            cls = tile_class(j)
            off = tile_offset(j)[:, :1]
            vv = v_ref[0, 0, pl.ds(start, t), :]
            for m in range(2):
                kk = k_ref[0, 0, m, pl.ds(start, t), :]
                s = lax.dot_general(q_ref[0, 0, m], kk, _NT_DIMS, preferred_element_type=F32)
                s = s + bias_ref[cls] + off
                m_prev = m_ref[m]
                m_new = jnp.maximum(m_prev, jnp.max(s, axis=-1, keepdims=True))
                alpha = jnp.exp2(m_prev - m_new)
                p = jnp.exp2(s - m_new)
                ls_ref[m] = alpha * ls_ref[m] + jnp.sum(p, axis=-1, keepdims=True)
                acc_ref[m] = alpha * acc_ref[m] + jnp.dot(p.astype(BF16), vv,
                                                          preferred_element_type=F32)
                m_ref[m] = m_new
            return carry

        lax.fori_loop(lo, hi, online_step, 0)

    lam = (jnp.exp(jnp.sum(lq1_ref[...] * lk1_ref[...], axis=-1, keepdims=True))
           - jnp.exp(jnp.sum(lq2_ref[...] * lk2_ref[...], axis=-1, keepdims=True))
           + lambda_init)
    o = acc_ref[0] / ls_ref[0] - lam * (acc_ref[1] / ls_ref[1])
    o = jnp.where(valid, o, 0.0)
    o = _rmsnorm(o, subg_ref[...]) * (1.0 - lambda_init)
    o_ref[0] = o.astype(BF16)


def _attention_layer(q, k, v, slopes, lq1, lk1, lq2, lk2, subg, seq_len, lambda_init):
    bsz, n_heads, _, lp, dh = q.shape
    tq = SEQ_TILE
    n_q = lp // tq
    dv = v.shape[-1]
    kv_bytes = 2 * (2 * lp * dh * 2 + lp * dv * 2)
    scratch = (N_BIAS_CLASSES * tq * tq + 4 * tq * LANES + 2 * tq * dv + 4 * tq * LANES) * 4
    blocks = kv_bytes + 2 * 2 * tq * dh * 2 + 2 * tq * dv * 2 + scratch
    vec = lambda b, h, i, *_: (0, 0)
    grid_spec = pltpu.PrefetchScalarGridSpec(
        num_scalar_prefetch=1,
        grid=(bsz, n_heads, n_q),
        in_specs=[
            pl.BlockSpec((1, 1, 2, tq, dh), lambda b, h, i, *_: (b, h, 0, i, 0)),
            pl.BlockSpec((1, 1, 2, lp, dh), lambda b, h, i, *_: (b, h, 0, 0, 0)),
            pl.BlockSpec((1, 1, lp, dv), lambda b, h, i, *_: (b, h, 0, 0)),
            pl.BlockSpec((1, dh), vec), pl.BlockSpec((1, dh), vec),
            pl.BlockSpec((1, dh), vec), pl.BlockSpec((1, dh), vec),
            pl.BlockSpec((1, dv), vec),
        ],
        out_specs=pl.BlockSpec((1, tq, dv), lambda b, h, i, *_: (b, i, h)),
        scratch_shapes=[
            pltpu.VMEM((N_BIAS_CLASSES, tq, tq), F32),
            pltpu.VMEM((2, 1, LANES), F32),
            pltpu.VMEM((2, tq, LANES), F32),
            pltpu.VMEM((2, tq, LANES), F32),
            pltpu.VMEM((2, tq, dv), F32),
            pltpu.VMEM((2, tq, 1), F32),
            pltpu.VMEM((2, tq, 1), F32),
        ],
    )
    return pl.pallas_call(
        functools.partial(_attn_kernel, seq_len=seq_len, lambda_init=lambda_init),
        out_shape=jax.ShapeDtypeStruct((bsz, lp, n_heads * dv), BF16),
        grid_spec=grid_spec,
        compiler_params=pltpu.CompilerParams(
            dimension_semantics=("arbitrary", "arbitrary", "arbitrary"),
            vmem_limit_bytes=_vmem_limit(blocks + 6 * tq * tq * 4)),
        name="diff_attention",
    )(slopes, q, k, v, lq1.reshape(1, dh), lk1.reshape(1, dh), lq2.reshape(1, dh),
      lk2.reshape(1, dh), subg.reshape(1, dv))


def _oproj_kernel(h_ref, o_ref, w_ref, out_ref):
    out_ref[...] = h_ref[...] + jnp.dot(o_ref[...], w_ref[...], preferred_element_type=F32)


def _oproj_layer(h, o, w_bf16):
    n, d = h.shape
    tm = TOKEN_TILE
    blocks = 4 * tm * d * 4 + 2 * tm * d * 2 + 2 * d * d * 2
    return pl.pallas_call(
        _oproj_kernel,
        out_shape=jax.ShapeDtypeStruct((n, d), F32),
        grid=(n // tm,),
        in_specs=[
            pl.BlockSpec((tm, d), lambda i: (i, 0)),
            pl.BlockSpec((tm, d), lambda i: (i, 0)),
            pl.BlockSpec((d, d), lambda i: (0, 0)),
        ],
        out_specs=pl.BlockSpec((tm, d), lambda i: (i, 0)),
        compiler_params=pltpu.CompilerParams(
            dimension_semantics=("parallel",),
            vmem_limit_bytes=_vmem_limit(blocks)),
        name="attn_out_proj",
    )(h, o, w_bf16)


def _encode(x, p):
    bsz, s, d = x.shape
    seq_len = s + N_META
    lp = -(-seq_len // SEQ_TILE) * SEQ_TILE
    meta = jnp.broadcast_to(p["meta_tokens"][None], (bsz, N_META, d))
    h = jnp.concatenate([meta, x, jnp.zeros((bsz, lp - seq_len, d), x.dtype)], axis=1)

    h = _pool_layer(h, p["mixer_norm_g"][0], p["pool_w"][0], p["pool_scale"][0], seq_len)
    h = h.reshape(bsz * lp, d)
    h = _ffn_layer(h, p["ffn_norm_g"][0], p["w_gate"][0], p["w_up"][0], p["w_down"][0],
                   p["final_norm_g"], final_norm=False)

    lambda_init = 0.8 - 0.6 * math.exp(-0.3 * 1)
    slopes = 2.0 ** (-8.0 * (jnp.arange(N_HEADS, dtype=F32) + 1.0) / N_HEADS)
    q, k, v = _qkv_layer(h, p["mixer_norm_g"][1], p["w_qkv"][0], bsz, lp)
    o = _attention_layer(q, k, v, slopes, p["lambda_q1"][0], p["lambda_k1"][0],
                         p["lambda_q2"][0], p["lambda_k2"][0], p["subln_g"][0], seq_len,
                         lambda_init)
    h = _oproj_layer(h, o.reshape(bsz * lp, d), p["w_o"][0])
    h = _ffn_layer(h, p["ffn_norm_g"][1], p["w_gate"][1], p["w_up"][1], p["w_down"][1],
                   p["final_norm_g"], final_norm=True)
    return h.reshape(bsz, lp, d)[:, N_META:seq_len]


def kernel(x_prompt, x_sample, meta_tokens, mixer_norm_g, pool_w, pool_scale, w_qkv, lambda_q1,
           lambda_k1, lambda_q2, lambda_k2, subln_g, w_o, ffn_norm_g, w_gate, w_up, w_down,
           final_norm_g):
    p = dict(
        meta_tokens=meta_tokens, mixer_norm_g=mixer_norm_g, pool_scale=pool_scale,
        lambda_q1=lambda_q1, lambda_k1=lambda_k1, lambda_q2=lambda_q2, lambda_k2=lambda_k2,
        subln_g=subln_g, ffn_norm_g=ffn_norm_g, final_norm_g=final_norm_g,
        pool_w=pool_w.astype(BF16), w_qkv=w_qkv.astype(BF16), w_o=w_o.astype(BF16),
        w_gate=w_gate.astype(BF16), w_up=w_up.astype(BF16), w_down=w_down.astype(BF16),
    )
    return (_encode(x_prompt, p), _encode(x_sample, p))
```

```python
import functools
import math

import jax
import jax.numpy as jnp
from jax import lax
from jax.experimental import pallas as pl
from jax.experimental.pallas import tpu as pltpu

D_MODEL = 2048
N_META = 16
N_POOL_GROUPS = 4
POOL_WINDOWS = (2, 4, 8, 16)
POOL_GROUP_DIM = D_MODEL // N_POOL_GROUPS
N_HEADS = 8
HEAD_DIM = 128
V_HEAD_DIM = 2 * HEAD_DIM
D_FF = 5632
RMS_EPS = 1e-6

F32 = jnp.float32
BF16 = jnp.bfloat16

LANES = 128
V7X_VMEM_BYTES = 64 * 1024 * 1024
V7X_VMEM_COMPILER_RESERVE = 8 * 1024 * 1024

SEQ_TILE = 512
FRONT_PAD = SEQ_TILE - N_META
TOKEN_TILE = 512
FF_TILE = 512
POOL_HALO = 8

LOG2_E = math.log2(math.e)
QK_LOG2_SCALE = HEAD_DIM ** -0.5 * LOG2_E
SKIP_LOG2_THRESHOLD = 90.0 * LOG2_E
MASKED_LOGIT = -1e30
MIN_BOUNDED_ROW_SUM = 2.0 ** -60
BOUND_SLACK = 1.0 + 2.0 ** -6


def _vmem_limit(block_bytes):
    return min(V7X_VMEM_BYTES - V7X_VMEM_COMPILER_RESERVE,
               int(block_bytes) + V7X_VMEM_COMPILER_RESERVE)


def _rmsnorm(x, g):
    ms = jnp.mean(x * x, axis=-1, keepdims=True)
    return x * lax.rsqrt(ms + RMS_EPS) * g


def _pool_kernel(cur_ref, prev_ref, next_ref, front_ref, g_ref, w_ref, scale_ref, out_ref, *,
                 n_pos):
    i = pl.program_id(1)
    n_tiles = pl.num_programs(1)
    tile = out_ref.shape[1]
    x = jnp.where(i == 0, front_ref[...], cur_ref[0])
    front_tail = front_ref[tile - POOL_HALO:, :]
    halo_lo = jnp.where(i >= 2, prev_ref[0], jnp.where(i == 1, front_tail, 0.0))
    halo_hi = jnp.where(i < n_tiles - 1, next_ref[0], 0.0)
    ext = jnp.concatenate([halo_lo, x, halo_hi], axis=0)
    hn = _rmsnorm(ext, g_ref[...])
    n_ext = ext.shape[0]

    def shifted(a, d):
        return pltpu.roll(a, (-d) % n_ext, axis=0)

    pos = i * tile - FRONT_PAD + lax.broadcasted_iota(jnp.int32, (tile, 1), 0)
    valid = pos >= 0
    for gi, w in enumerate(POOL_WINDOWS):
        cols = slice(gi * POOL_GROUP_DIM, (gi + 1) * POOL_GROUP_DIM)
        e = hn[:, cols]
        s = shifted(e, -1) + e
        reach = 1
        while 2 * reach < w:
            s = shifted(s, -reach) + shifted(s, reach)
            reach *= 2
        win = s[POOL_HALO:POOL_HALO + tile]
        cnt = jnp.minimum(pos + w // 2, n_pos) - jnp.maximum(pos - w // 2, 0)
        cnt = jnp.maximum(cnt, 1).astype(F32)
        pooled = win / cnt - e[POOL_HALO:POOL_HALO + tile]
        y = jnp.dot(pooled.astype(BF16), w_ref[gi], preferred_element_type=F32)
        y = y * scale_ref[:, cols]
        out_ref[0, :, cols] = jnp.where(valid, x[:, cols] + y, 0.0)


def _pool_layer(x, front, g, w_bf16, scale):
    bsz, s, d = x.shape
    tile = SEQ_TILE
    nt = s // tile + 1
    hb = tile // POOL_HALO
    n_halo_blocks = s // POOL_HALO
    blk = tile * d * 4
    return pl.pallas_call(
        functools.partial(_pool_kernel, n_pos=s + N_META),
        out_shape=jax.ShapeDtypeStruct((bsz, nt * tile, d), F32),
        grid=(bsz, nt),
        in_specs=[
            pl.BlockSpec((1, tile, d), lambda b, i: (b, jnp.maximum(i - 1, 0), 0)),
            pl.BlockSpec((1, POOL_HALO, d),
                         lambda b, i: (b, jnp.maximum((i - 1) * hb - 1, 0), 0)),
            pl.BlockSpec((1, POOL_HALO, d),
                         lambda b, i: (b, jnp.minimum(i * hb, n_halo_blocks - 1), 0)),
            pl.BlockSpec((tile, d), lambda b, i: (0, 0)),
            pl.BlockSpec((1, d), lambda b, i: (0, 0)),
            pl.BlockSpec((N_POOL_GROUPS, POOL_GROUP_DIM, POOL_GROUP_DIM), lambda b, i: (0, 0, 0)),
            pl.BlockSpec((1, d), lambda b, i: (0, 0)),
        ],
        out_specs=pl.BlockSpec((1, tile, d), lambda b, i: (b, i, 0)),
        compiler_params=pltpu.CompilerParams(
            dimension_semantics=("parallel", "arbitrary"),
            vmem_limit_bytes=_vmem_limit(6 * blk + 4 * w_bf16.size + 4 * blk)),
        name="pool_mixer",
    )(x, x, x, front, g.reshape(1, d), w_bf16, scale.reshape(1, d))


def _ffn_kernel(*refs, with_attn, final_norm):
    if with_attn:
        x_ref, a_ref, wo_ref, g_ref, wg_ref, wu_ref, wd_ref, gf_ref, o_ref, hn_ref = refs
    else:
        x_ref, g_ref, wg_ref, wu_ref, wd_ref, gf_ref, o_ref, hn_ref = refs
    c = pl.program_id(1)

    @pl.when(c == 0)
    def _():
        x = x_ref[...]
        if with_attn:
            x = x + jnp.dot(a_ref[...], wo_ref[...], preferred_element_type=F32)
        hn_ref[...] = _rmsnorm(x, g_ref[...]).astype(BF16)
        o_ref[...] = x

    hn = hn_ref[...]
    gate = jnp.dot(hn, wg_ref[...], preferred_element_type=F32)
    up = jnp.dot(hn, wu_ref[...], preferred_element_type=F32)
    act = (gate * jax.nn.sigmoid(gate) * up).astype(BF16)
    o_ref[...] += jnp.dot(act, wd_ref[...], preferred_element_type=F32)

    if final_norm:
        @pl.when(c == pl.num_programs(1) - 1)
        def _():
            o_ref[...] = _rmsnorm(o_ref[...], gf_ref[...])


def _ffn_layer(x, g, wg, wu, wd, gf, *, n_out, x_tile_of, attn=None, w_o=None,
               final_norm=False):
    d = x.shape[1]
    tm, tf = TOKEN_TILE, FF_TILE
    dff = wg.shape[1]
    with_attn = attn is not None
    blocks = 4 * tm * d * 4 + tm * d * 2 + 2 * 3 * d * tf * 2 + 3 * tm * tf * 4
    in_specs = [pl.BlockSpec((tm, d), lambda i, c: (x_tile_of(i), 0))]
    args = [x]
    if with_attn:
        in_specs += [pl.BlockSpec((tm, d), lambda i, c: (i, 0)),
                     pl.BlockSpec((d, d), lambda i, c: (0, 0), pipeline_mode=pl.Buffered(1))]
        args += [attn, w_o]
        blocks += 2 * tm * d * 2 + d * d * 2
    in_specs += [
        pl.BlockSpec((1, d), lambda i, c: (0, 0)),
        pl.BlockSpec((d, tf), lambda i, c: (0, c)),
        pl.BlockSpec((d, tf), lambda i, c: (0, c)),
        pl.BlockSpec((tf, d), lambda i, c: (c, 0)),
        pl.BlockSpec((1, d), lambda i, c: (0, 0)),
    ]
    args += [g.reshape(1, d), wg, wu, wd, gf.reshape(1, d)]
    return pl.pallas_call(
        functools.partial(_ffn_kernel, with_attn=with_attn, final_norm=final_norm),
        out_shape=jax.ShapeDtypeStruct((n_out, d), F32),
        grid=(n_out // tm, dff // tf),
        in_specs=in_specs,
        out_specs=pl.BlockSpec((tm, d), lambda i, c: (i, 0)),
        scratch_shapes=[pltpu.VMEM((tm, d), BF16)],
        compiler_params=pltpu.CompilerParams(
            dimension_semantics=("parallel", "arbitrary"),
            vmem_limit_bytes=_vmem_limit(blocks)),
        name="oproj_swiglu_ffn_final" if with_attn else "swiglu_ffn",
    )(*args)


def _qkv_kernel(x_ref, g_ref, w_ref, q_ref, k_ref, v_ref):
    hn = _rmsnorm(x_ref[...], g_ref[...]).astype(BF16)
    for h in range(N_HEADS):
        for part, ref in ((0, q_ref), (1, k_ref)):
            c0 = part * D_MODEL + h * V_HEAD_DIM
            r = jnp.dot(hn, w_ref[:, c0:c0 + V_HEAD_DIM], preferred_element_type=F32)
            if part == 0:
                r = r * QK_LOG2_SCALE
            ref[0, h, 0] = r[:, :HEAD_DIM].astype(BF16)
            ref[0, h, 1] = r[:, HEAD_DIM:].astype(BF16)
        c0 = 2 * D_MODEL + h * V_HEAD_DIM
        v_ref[0, h] = jnp.dot(hn, w_ref[:, c0:c0 + V_HEAD_DIM],
                              preferred_element_type=F32).astype(BF16)


def _qkv_layer(x, g, w_bf16, bsz, lp):
    n, d = x.shape
    tm = TOKEN_TILE
    nt = lp // tm
    qk_shape = jax.ShapeDtypeStruct((bsz, N_HEADS, 2, lp, HEAD_DIM), BF16)
    v_shape = jax.ShapeDtypeStruct((bsz, N_HEADS, lp, V_HEAD_DIM), BF16)
    blocks = 2 * tm * d * 4 + w_bf16.size * 2 + 2 * 3 * tm * d * 2 + tm * d * 2
    return pl.pallas_call(
        _qkv_kernel,
        out_shape=(qk_shape, qk_shape, v_shape),
        grid=(n // tm,),
        in_specs=[
            pl.BlockSpec((tm, d), lambda i: (i, 0)),
            pl.BlockSpec((1, d), lambda i: (0, 0)),
            pl.BlockSpec((d, 3 * d), lambda i: (0, 0), pipeline_mode=pl.Buffered(1)),
        ],
        out_specs=(
            pl.BlockSpec((1, N_HEADS, 2, tm, HEAD_DIM), lambda i: (i // nt, 0, 0, i % nt, 0)),
            pl.BlockSpec((1, N_HEADS, 2, tm, HEAD_DIM), lambda i: (i // nt, 0, 0, i % nt, 0)),
            pl.BlockSpec((1, N_HEADS, tm, V_HEAD_DIM), lambda i: (i // nt, 0, i % nt, 0)),
        ),
        compiler_params=pltpu.CompilerParams(
            dimension_semantics=("parallel",),
            vmem_limit_bytes=_vmem_limit(blocks)),
        name="qkv_proj",
    )(x, g.reshape(1, d), w_bf16)


BIAS_LEFT, BIAS_DIAG, BIAS_RIGHT, BIAS_FRONT = 0, 1, 2, 3
N_BIAS_CLASSES = 4
_NT_DIMS = (((1,), (1,)), ((), ()))


def _attn_kernel(slope_ref, q_ref, k_ref, v_ref, lq1_ref, lk1_ref, lq2_ref, lk2_ref, subg_ref,
                 o_ref, bias_ref, kmax_ref, bound_ref, l_ref, acc_ref, m_ref, ls_ref, *,
                 lambda_init):
    h = pl.program_id(1)
    qi = pl.program_id(2) + 1
    t = q_ref.shape[3]
    n_rows = k_ref.shape[3]
    n_tiles = n_rows // t
    n_chunks = t // LANES
    sigma = slope_ref[h] * LOG2_E

    @pl.when(qi == 1)
    def _():
        r = lax.broadcasted_iota(jnp.int32, (t, t), 0)
        c = lax.broadcasted_iota(jnp.int32, (t, t), 1)
        left = -sigma * (r - c).astype(F32)
        bias_ref[BIAS_LEFT] = left
        bias_ref[BIAS_DIAG] = -jnp.abs(left)
        bias_ref[BIAS_RIGHT] = -left
        bias_ref[BIAS_FRONT] = jnp.where(c >= FRONT_PAD, left, MASKED_LOGIT)
        for m in range(2):
            def widest(j, best, m=m):
                kk = k_ref[0, 0, m, pl.ds(pl.multiple_of(j * t, t), t), :].astype(F32)
                return jnp.maximum(best, jnp.sum(kk * kk, axis=-1, keepdims=True))
            best = lax.fori_loop(0, n_tiles, widest, jnp.zeros((t, 1), F32))
            kmax_ref[m] = jnp.broadcast_to(jnp.sqrt(jnp.max(best, axis=0, keepdims=True)),
                                           (1, LANES))

    products = []
    for m in range(2):
        qf = q_ref[0, 0, m].astype(F32)
        norm = jnp.sqrt(jnp.sum(qf * qf, axis=-1, keepdims=True))
        kmax = kmax_ref[m]
        bound_ref[m] = norm * (kmax * BOUND_SLACK)
        products.append(jnp.max(norm, axis=0, keepdims=True) * kmax)
    l_ref[...] = jnp.zeros(l_ref.shape, F32)
    acc_ref[...] = jnp.zeros(acc_ref.shape, F32)

    spread = 2.0 * BOUND_SLACK * jnp.maximum(products[0], products[1]) + 1.0
    reach = (SKIP_LOG2_THRESHOLD + spread) / sigma
    reach = jnp.ceil(jnp.minimum(reach, float(n_rows))) + 1.0
    first_row = jnp.full((1, LANES), qi * t, jnp.int32).astype(F32)
    lo_tile = jnp.floor(jnp.maximum(first_row - reach, 0.0) * (1.0 / t))
    hi_tile = jnp.floor(jnp.minimum(first_row + (t - 1.0) + reach, n_rows - 1.0) * (1.0 / t))
    lo = jnp.max(lo_tile).astype(jnp.int32)
    hi = jnp.max(hi_tile).astype(jnp.int32) + 1

    def tile_class(j):
        side = jnp.where(j < qi, BIAS_LEFT, jnp.where(j == qi, BIAS_DIAG, BIAS_RIGHT))
        return jnp.where(j == 0, BIAS_FRONT, side)

    def tile_offset(j):
        gap = jnp.full((1, LANES), jnp.abs(qi - j) * t, jnp.int32).astype(F32)
        return gap * (-sigma)

    def bounded_block(j0, n_blk):
        start = pl.multiple_of(j0 * t, t)
        width = n_blk * t
        classes = [tile_class(j0 + u) for u in range(n_blk)]
        offsets = [tile_offset(j0 + u) for u in range(n_blk)]
        vv = v_ref[0, 0, pl.ds(start, width), :]
        for m in range(2):
            kk = k_ref[0, 0, m, pl.ds(start, width), :]
            s = lax.dot_general(q_ref[0, 0, m], kk, _NT_DIMS, preferred_element_type=F32)
            row_sum = None
            ps = []
            for u in range(n_blk):
                shift = offsets[u] - bound_ref[m]
                for cc in range(n_chunks):
                    sl = slice(cc * LANES, (cc + 1) * LANES)
                    su = s[:, u * t + cc * LANES:u * t + (cc + 1) * LANES]
                    p = jnp.exp2(su + bias_ref[classes[u], :, sl] + shift)
                    row_sum = p if row_sum is None else row_sum + p
                    ps.append(p.astype(BF16))
            l_ref[m] += row_sum
            acc_ref[m] += jnp.dot(jnp.concatenate(ps, axis=1), vv, preferred_element_type=F32)

    n_steps = hi - lo
    n_wide = lax.shift_right_logical(n_steps, 2)

    def wide_body(i, carry):
        bounded_block(lo + 4 * i, 4)
        return carry

    lax.fori_loop(0, n_wide, wide_body, 0)
    rest = lo + 4 * n_wide

    @pl.when(jnp.bitwise_and(n_steps, 2) != 0)
    def _():
        bounded_block(rest, 2)

    @pl.when(jnp.bitwise_and(n_steps, 1) != 0)
    def _():
        bounded_block(hi - 1, 1)

    for m in range(2):
        ls_ref[m] = jnp.sum(l_ref[m], axis=-1, keepdims=True)
    smallest = jnp.min(jnp.minimum(ls_ref[0], ls_ref[1]))

    @pl.when(smallest < MIN_BOUNDED_ROW_SUM)
    def _():
        m_ref[...] = jnp.full(m_ref.shape, MASKED_LOGIT, F32)
        ls_ref[...] = jnp.zeros(ls_ref.shape, F32)
        acc_ref[...] = jnp.zeros(acc_ref.shape, F32)

        def online_step(j, carry):
            start = pl.multiple_of(j * t, t)
            cls = tile_class(j)
            off = tile_offset(j)[:, :1]
            vv = v_ref[0, 0, pl.ds(start, t), :]
            for m in range(2):
                kk = k_ref[0, 0, m, pl.ds(start, t), :]
                s = lax.dot_general(q_ref[0, 0, m], kk, _NT_DIMS, preferred_element_type=F32)
                s = s + bias_ref[cls] + off
                m_prev = m_ref[m]
                m_new = jnp.maximum(m_prev, jnp.max(s, axis=-1, keepdims=True))
                alpha = jnp.exp2(m_prev - m_new)
                p = jnp.exp2(s - m_new)
                ls_ref[m] = alpha * ls_ref[m] + jnp.sum(p, axis=-1, keepdims=True)
                acc_ref[m] = alpha * acc_ref[m] + jnp.dot(p.astype(BF16), vv,
                                                          preferred_element_type=F32)
                m_ref[m] = m_new
            return carry

        lax.fori_loop(lo, hi, online_step, 0)

    lam = (jnp.exp(jnp.sum(lq1_ref[...] * lk1_ref[...], axis=-1, keepdims=True))
           - jnp.exp(jnp.sum(lq2_ref[...] * lk2_ref[...], axis=-1, keepdims=True))
           + lambda_init)
    o = acc_ref[0] / ls_ref[0] - lam * (acc_ref[1] / ls_ref[1])
    o = _rmsnorm(o, subg_ref[...]) * (1.0 - lambda_init)
    o_ref[0] = o.astype(BF16)


def _attention_layer(q, k, v, slopes, lq1, lk1, lq2, lk2, subg, lambda_init):
    bsz, n_heads, _, lp, dh = q.shape
    tq = SEQ_TILE
    n_q = lp // tq - 1
    dv = v.shape[-1]
    kv_bytes = 2 * (2 * lp * dh * 2 + lp * dv * 2)
    scratch = (N_BIAS_CLASSES * tq * tq + 4 * tq * LANES + 2 * tq * dv + 4 * tq * LANES) * 4
    blocks = kv_bytes + 2 * 2 * tq * dh * 2 + 2 * tq * dv * 2 + scratch
    vec = lambda b, h, i, *_: (0, 0)
    grid_spec = pltpu.PrefetchScalarGridSpec(
        num_scalar_prefetch=1,
        grid=(bsz, n_heads, n_q),
        in_specs=[
            pl.BlockSpec((1, 1, 2, tq, dh), lambda b, h, i, *_: (b, h, 0, i + 1, 0)),
            pl.BlockSpec((1, 1, 2, lp, dh), lambda b, h, i, *_: (b, h, 0, 0, 0)),
            pl.BlockSpec((1, 1, lp, dv), lambda b, h, i, *_: (b, h, 0, 0)),
            pl.BlockSpec((1, dh), vec), pl.BlockSpec((1, dh), vec),
            pl.BlockSpec((1, dh), vec), pl.BlockSpec((1, dh), vec),
            pl.BlockSpec((1, dv), vec),
        ],
        out_specs=pl.BlockSpec((1, tq, dv), lambda b, h, i, *_: (b, i, h)),
        scratch_shapes=[
            pltpu.VMEM((N_BIAS_CLASSES, tq, tq), F32),
            pltpu.VMEM((2, 1, LANES), F32),
            pltpu.VMEM((2, tq, LANES), F32),
            pltpu.VMEM((2, tq, LANES), F32),
            pltpu.VMEM((2, tq, dv), F32),
            pltpu.VMEM((2, tq, 1), F32),
            pltpu.VMEM((2, tq, 1), F32),
        ],
    )
    return pl.pallas_call(
        functools.partial(_attn_kernel, lambda_init=lambda_init),
        out_shape=jax.ShapeDtypeStruct((bsz, n_q * tq, n_heads * dv), BF16),
        grid_spec=grid_spec,
        compiler_params=pltpu.CompilerParams(
            dimension_semantics=("arbitrary", "arbitrary", "arbitrary"),
            vmem_limit_bytes=_vmem_limit(blocks + 6 * 4 * tq * tq * 4)),
        name="diff_attention",
    )(slopes, q, k, v, lq1.reshape(1, dh), lk1.reshape(1, dh), lq2.reshape(1, dh),
      lk2.reshape(1, dh), subg.reshape(1, dv))


def _encode(x, p):
    bsz, s, d = x.shape
    assert s % SEQ_TILE == 0 and SEQ_TILE == TOKEN_TILE
    lp = s + SEQ_TILE
    tiles_in, tiles_out = lp // TOKEN_TILE, s // TOKEN_TILE
    front = jnp.concatenate([jnp.zeros((FRONT_PAD, d), x.dtype), p["meta_tokens"]], axis=0)

    h = _pool_layer(x, front, p["mixer_norm_g"][0], p["pool_w"][0], p["pool_scale"][0])
    h = h.reshape(bsz * lp, d)
    h = _ffn_layer(h, p["ffn_norm_g"][0], p["w_gate"][0], p["w_up"][0], p["w_down"][0],
                   p["final_norm_g"], n_out=bsz * lp, x_tile_of=lambda i: i)

    lambda_init = 0.8 - 0.6 * math.exp(-0.3 * 1)
    slopes = 2.0 ** (-8.0 * (jnp.arange(N_HEADS, dtype=F32) + 1.0) / N_HEADS)
    q, k, v = _qkv_layer(h, p["mixer_norm_g"][1], p["w_qkv"][0], bsz, lp)
    o = _attention_layer(q, k, v, slopes, p["lambda_q1"][0], p["lambda_k1"][0],
                         p["lambda_q2"][0], p["lambda_k2"][0], p["subln_g"][0], lambda_init)
    y = _ffn_layer(h, p["ffn_norm_g"][1], p["w_gate"][1], p["w_up"][1], p["w_down"][1],
                   p["final_norm_g"], n_out=bsz * s,
                   x_tile_of=lambda i: (i // tiles_out) * tiles_in + i % tiles_out + 1,
                   attn=o.reshape(bsz * s, d), w_o=p["w_o"][0], final_norm=True)
    return y.reshape(bsz, s, d)


def kernel(x_prompt, x_sample, meta_tokens, mixer_norm_g, pool_w, pool_scale, w_qkv, lambda_q1,
           lambda_k1, lambda_q2, lambda_k2, subln_g, w_o, ffn_norm_g, w_gate, w_up, w_down,
           final_norm_g):
    p = dict(
        meta_tokens=meta_tokens, mixer_norm_g=mixer_norm_g, pool_scale=pool_scale,
        lambda_q1=lambda_q1, lambda_k1=lambda_k1, lambda_q2=lambda_q2, lambda_k2=lambda_k2,
        subln_g=subln_g, ffn_norm_g=ffn_norm_g, final_norm_g=final_norm_g,
        pool_w=pool_w.astype(BF16), w_qkv=w_qkv.astype(BF16), w_o=w_o.astype(BF16),
        w_gate=w_gate.astype(BF16), w_up=w_up.astype(BF16), w_down=w_down.astype(BF16),
    )
    return (_encode(x_prompt, p), _encode(x_sample, p))
```

```python
import functools
import math

import jax
import jax.numpy as jnp
from jax import lax
from jax.experimental import pallas as pl
from jax.experimental.pallas import tpu as pltpu

D_MODEL = 2048
N_META = 16
N_POOL_GROUPS = 4
POOL_WINDOWS = (2, 4, 8, 16)
POOL_GROUP_DIM = D_MODEL // N_POOL_GROUPS
N_HEADS = 8
HEAD_DIM = 128
V_HEAD_DIM = 2 * HEAD_DIM
D_FF = 5632
RMS_EPS = 1e-6

F32 = jnp.float32
BF16 = jnp.bfloat16

LANES = 128
V7X_VMEM_BYTES = 64 * 1024 * 1024
V7X_VMEM_COMPILER_RESERVE = 8 * 1024 * 1024

SEQ_TILE = 512
FRONT_PAD = SEQ_TILE - N_META
TOKEN_TILE = 512
WIDE_TOKEN_TILE = 1024
FF_TILE = 512
POOL_HALO = 8

LOG2_E = math.log2(math.e)
QK_LOG2_SCALE = HEAD_DIM ** -0.5 * LOG2_E
SKIP_LOG2_THRESHOLD = 90.0 * LOG2_E
MASKED_LOGIT = -1e30
MIN_BOUNDED_ROW_SUM = 2.0 ** -60
BOUND_SLACK = 1.0 + 2.0 ** -6


def _vmem_limit(block_bytes):
    return min(V7X_VMEM_BYTES - V7X_VMEM_COMPILER_RESERVE,
               int(block_bytes) + V7X_VMEM_COMPILER_RESERVE)


def _rmsnorm(x, g):
    ms = jnp.mean(x * x, axis=-1, keepdims=True)
    return x * lax.rsqrt(ms + RMS_EPS) * g


def _pool_kernel(cur_ref, prev_ref, next_ref, front_ref, g_ref, w_ref, scale_ref, out_ref, *,
                 n_pos):
    i = pl.program_id(1)
    n_tiles = pl.num_programs(1)
    tile = out_ref.shape[1]
    x = jnp.where(i == 0, front_ref[...], cur_ref[0])
    front_tail = front_ref[tile - POOL_HALO:, :]
    halo_lo = jnp.where(i >= 2, prev_ref[0], jnp.where(i == 1, front_tail, 0.0))
    halo_hi = jnp.where(i < n_tiles - 1, next_ref[0], 0.0)
    ext = jnp.concatenate([halo_lo, x, halo_hi], axis=0)
    hn = _rmsnorm(ext, g_ref[...])
    n_ext = ext.shape[0]

    def shifted(a, d):
        return pltpu.roll(a, (-d) % n_ext, axis=0)

    pos = i * tile - FRONT_PAD + lax.broadcasted_iota(jnp.int32, (tile, 1), 0)
    valid = pos >= 0
    for gi, w in enumerate(POOL_WINDOWS):
        cols = slice(gi * POOL_GROUP_DIM, (gi + 1) * POOL_GROUP_DIM)
        e = hn[:, cols]
        s = shifted(e, -1) + e
        reach = 1
        while 2 * reach < w:
            s = shifted(s, -reach) + shifted(s, reach)
            reach *= 2
        win = s[POOL_HALO:POOL_HALO + tile]
        cnt = jnp.minimum(pos + w // 2, n_pos) - jnp.maximum(pos - w // 2, 0)
        cnt = jnp.maximum(cnt, 1).astype(F32)
        pooled = win / cnt - e[POOL_HALO:POOL_HALO + tile]
        y = jnp.dot(pooled.astype(BF16), w_ref[gi], preferred_element_type=F32)
        y = y * scale_ref[:, cols]
        out_ref[0, :, cols] = jnp.where(valid, x[:, cols] + y, 0.0)


def _pool_layer(x, front, g, w_bf16, scale):
    bsz, s, d = x.shape
    tile = SEQ_TILE
    nt = s // tile + 1
    hb = tile // POOL_HALO
    n_halo_blocks = s // POOL_HALO
    blk = tile * d * 4
    return pl.pallas_call(
        functools.partial(_pool_kernel, n_pos=s + N_META),
        out_shape=jax.ShapeDtypeStruct((bsz, nt * tile, d), F32),
        grid=(bsz, nt),
        in_specs=[
            pl.BlockSpec((1, tile, d), lambda b, i: (b, jnp.maximum(i - 1, 0), 0)),
            pl.BlockSpec((1, POOL_HALO, d),
                         lambda b, i: (b, jnp.maximum((i - 1) * hb - 1, 0), 0)),
            pl.BlockSpec((1, POOL_HALO, d),
                         lambda b, i: (b, jnp.minimum(i * hb, n_halo_blocks - 1), 0)),
            pl.BlockSpec((tile, d), lambda b, i: (0, 0)),
            pl.BlockSpec((1, d), lambda b, i: (0, 0)),
            pl.BlockSpec((N_POOL_GROUPS, POOL_GROUP_DIM, POOL_GROUP_DIM), lambda b, i: (0, 0, 0)),
            pl.BlockSpec((1, d), lambda b, i: (0, 0)),
        ],
        out_specs=pl.BlockSpec((1, tile, d), lambda b, i: (b, i, 0)),
        compiler_params=pltpu.CompilerParams(
            dimension_semantics=("parallel", "arbitrary"),
            vmem_limit_bytes=_vmem_limit(6 * blk + 4 * w_bf16.size + 4 * blk)),
        name="pool_mixer",
    )(x, x, x, front, g.reshape(1, d), w_bf16, scale.reshape(1, d))


def _ffn_kernel(*refs, with_attn, final_norm):
    if with_attn:
        x_ref, a_ref, wo_ref, g_ref, wg_ref, wu_ref, wd_ref, gf_ref, o_ref, hn_ref = refs
    else:
        x_ref, g_ref, wg_ref, wu_ref, wd_ref, gf_ref, o_ref, hn_ref = refs
    c = pl.program_id(1)

    def ffn_chunk(hn):
        gate = jnp.dot(hn, wg_ref[...], preferred_element_type=F32)
        up = jnp.dot(hn, wu_ref[...], preferred_element_type=F32)
        act = (gate * jax.nn.sigmoid(gate) * up).astype(BF16)
        return jnp.dot(act, wd_ref[...], preferred_element_type=F32)

    @pl.when(c == 0)
    def _():
        x = x_ref[...]
        if with_attn:
            x = x + jnp.dot(a_ref[...], wo_ref[...], preferred_element_type=F32)
        hn = _rmsnorm(x, g_ref[...]).astype(BF16)
        hn_ref[...] = hn
        o_ref[...] = x + ffn_chunk(hn)

    @pl.when(c > 0)
    def _():
        o_ref[...] += ffn_chunk(hn_ref[...])

    if final_norm:
        @pl.when(c == pl.num_programs(1) - 1)
        def _():
            o_ref[...] = _rmsnorm(o_ref[...], gf_ref[...])


def _ffn_layer(x, g, wg, wu, wd, gf, *, n_out, x_tile_of, tm, attn=None, w_o=None,
               final_norm=False):
    d = x.shape[1]
    tf = FF_TILE
    dff = wg.shape[1]
    with_attn = attn is not None
    blocks = 4 * tm * d * 4 + tm * d * 2 + 2 * 3 * d * tf * 2 + 3 * tm * tf * 4
    in_specs = [pl.BlockSpec((tm, d), lambda i, c: (x_tile_of(i), 0))]
    args = [x]
    if with_attn:
        in_specs += [pl.BlockSpec((tm, d), lambda i, c: (i, 0)),
                     pl.BlockSpec((d, d), lambda i, c: (0, 0), pipeline_mode=pl.Buffered(1))]
        args += [attn, w_o]
        blocks += 2 * tm * d * 2 + d * d * 2
    in_specs += [
        pl.BlockSpec((1, d), lambda i, c: (0, 0)),
        pl.BlockSpec((d, tf), lambda i, c: (0, c)),
        pl.BlockSpec((d, tf), lambda i, c: (0, c)),
        pl.BlockSpec((tf, d), lambda i, c: (c, 0)),
        pl.BlockSpec((1, d), lambda i, c: (0, 0)),
    ]
    args += [g.reshape(1, d), wg, wu, wd, gf.reshape(1, d)]
    return pl.pallas_call(
        functools.partial(_ffn_kernel, with_attn=with_attn, final_norm=final_norm),
        out_shape=jax.ShapeDtypeStruct((n_out, d), F32),
        grid=(n_out // tm, dff // tf),
        in_specs=in_specs,
        out_specs=pl.BlockSpec((tm, d), lambda i, c: (i, 0)),
        scratch_shapes=[pltpu.VMEM((tm, d), BF16)],
        compiler_params=pltpu.CompilerParams(
            dimension_semantics=("parallel", "arbitrary"),
            vmem_limit_bytes=_vmem_limit(blocks)),
        name="oproj_swiglu_ffn_final" if with_attn else "swiglu_ffn",
    )(*args)


def _qkv_kernel(x_ref, g_ref, w_ref, q_ref, k_ref, v_ref):
    hn = _rmsnorm(x_ref[...], g_ref[...]).astype(BF16)
    for h in range(N_HEADS):
        for part, ref in ((0, q_ref), (1, k_ref)):
            c0 = part * D_MODEL + h * V_HEAD_DIM
            r = jnp.dot(hn, w_ref[:, c0:c0 + V_HEAD_DIM], preferred_element_type=F32)
            if part == 0:
                r = r * QK_LOG2_SCALE
            ref[0, h, 0] = r[:, :HEAD_DIM].astype(BF16)
            ref[0, h, 1] = r[:, HEAD_DIM:].astype(BF16)
        c0 = 2 * D_MODEL + h * V_HEAD_DIM
        v_ref[0, h] = jnp.dot(hn, w_ref[:, c0:c0 + V_HEAD_DIM],
                              preferred_element_type=F32).astype(BF16)


def _qkv_layer(x, g, w_bf16, bsz, lp):
    n, d = x.shape
    tm = TOKEN_TILE
    nt = lp // tm
    qk_shape = jax.ShapeDtypeStruct((bsz, N_HEADS, 2, lp, HEAD_DIM), BF16)
    v_shape = jax.ShapeDtypeStruct((bsz, N_HEADS, lp, V_HEAD_DIM), BF16)
    blocks = 2 * tm * d * 4 + w_bf16.size * 2 + 2 * 3 * tm * d * 2 + tm * d * 2
    return pl.pallas_call(
        _qkv_kernel,
        out_shape=(qk_shape, qk_shape, v_shape),
        grid=(n // tm,),
        in_specs=[
            pl.BlockSpec((tm, d), lambda i: (i, 0)),
            pl.BlockSpec((1, d), lambda i: (0, 0)),
            pl.BlockSpec((d, 3 * d), lambda i: (0, 0), pipeline_mode=pl.Buffered(1)),
        ],
        out_specs=(
            pl.BlockSpec((1, N_HEADS, 2, tm, HEAD_DIM), lambda i: (i // nt, 0, 0, i % nt, 0)),
            pl.BlockSpec((1, N_HEADS, 2, tm, HEAD_DIM), lambda i: (i // nt, 0, 0, i % nt, 0)),
            pl.BlockSpec((1, N_HEADS, tm, V_HEAD_DIM), lambda i: (i // nt, 0, i % nt, 0)),
        ),
        compiler_params=pltpu.CompilerParams(
            dimension_semantics=("parallel",),
            vmem_limit_bytes=_vmem_limit(blocks)),
        name="qkv_proj",
    )(x, g.reshape(1, d), w_bf16)


BIAS_LEFT, BIAS_DIAG, BIAS_RIGHT, BIAS_FRONT = 0, 1, 2, 3
N_BIAS_CLASSES = 4
_NT_DIMS = (((1,), (1,)), ((), ()))


def _attn_kernel(slope_ref, q_ref, k_ref, v_ref, lq1_ref, lk1_ref, lq2_ref, lk2_ref, subg_ref,
                 o_ref, bias_ref, kmax_ref, bound_ref, l_ref, acc_ref, m_ref, ls_ref, *,
                 lambda_init):
    h = pl.program_id(1)
    qi = pl.program_id(2) + 1
    t = q_ref.shape[3]
    n_rows = k_ref.shape[3]
    n_tiles = n_rows // t
    n_chunks = t // LANES
    sigma = slope_ref[h] * LOG2_E

    @pl.when(qi == 1)
    def _():
        r = lax.broadcasted_iota(jnp.int32, (t, t), 0)
        c = lax.broadcasted_iota(jnp.int32, (t, t), 1)
        left = -sigma * (r - c).astype(F32)
        bias_ref[BIAS_LEFT] = left
        bias_ref[BIAS_DIAG] = -jnp.abs(left)
        bias_ref[BIAS_RIGHT] = -left
        bias_ref[BIAS_FRONT] = jnp.where(c >= FRONT_PAD, left, MASKED_LOGIT)
        for m in range(2):
            def widest(j, best, m=m):
                kk = k_ref[0, 0, m, pl.ds(pl.multiple_of(j * t, t), t), :].astype(F32)
                return jnp.maximum(best, jnp.sum(kk * kk, axis=-1, keepdims=True))
            best = lax.fori_loop(0, n_tiles, widest, jnp.zeros((t, 1), F32))
            kmax_ref[m] = jnp.broadcast_to(jnp.sqrt(jnp.max(best, axis=0, keepdims=True)),
                                           (1, LANES))

    products = []
    for m in range(2):
        qf = q_ref[0, 0, m].astype(F32)
        norm = jnp.sqrt(jnp.sum(qf * qf, axis=-1, keepdims=True))
        kmax = kmax_ref[m]
        bound_ref[m] = norm * (kmax * BOUND_SLACK)
        products.append(jnp.max(norm, axis=0, keepdims=True) * kmax)
    l_ref[...] = jnp.zeros(l_ref.shape, F32)
    acc_ref[...] = jnp.zeros(acc_ref.shape, F32)

    spread = 2.0 * BOUND_SLACK * jnp.maximum(products[0], products[1]) + 1.0
    reach = (SKIP_LOG2_THRESHOLD + spread) / sigma
    reach = jnp.ceil(jnp.minimum(reach, float(n_rows))) + 1.0
    first_row = jnp.full((1, LANES), qi * t, jnp.int32).astype(F32)
    lo_tile = jnp.floor(jnp.maximum(first_row - reach, 0.0) * (1.0 / t))
    hi_tile = jnp.floor(jnp.minimum(first_row + (t - 1.0) + reach, n_rows - 1.0) * (1.0 / t))
    lo = jnp.max(lo_tile).astype(jnp.int32)
    hi = jnp.max(hi_tile).astype(jnp.int32) + 1

    def tile_class(j):
        side = jnp.where(j < qi, BIAS_LEFT, jnp.where(j == qi, BIAS_DIAG, BIAS_RIGHT))
        return jnp.where(j == 0, BIAS_FRONT, side)

    def tile_offset(j):
        gap = jnp.full((1, LANES), jnp.abs(qi - j) * t, jnp.int32).astype(F32)
        return gap * (-sigma)

    def bounded_block(j0, n_blk):
        start = pl.multiple_of(j0 * t, t)
        width = n_blk * t
        classes = [tile_class(j0 + u) for u in range(n_blk)]
        offsets = [tile_offset(j0 + u) for u in range(n_blk)]
        vv = v_ref[0, 0, pl.ds(start, width), :]
        for m in range(2):
            kk = k_ref[0, 0, m, pl.ds(start, width), :]
            s = lax.dot_general(q_ref[0, 0, m], kk, _NT_DIMS, preferred_element_type=F32)
            row_sum = None
            ps = []
            for u in range(n_blk):
                shift = offsets[u] - bound_ref[m]
                for cc in range(n_chunks):
                    sl = slice(cc * LANES, (cc + 1) * LANES)
                    su = s[:, u * t + cc * LANES:u * t + (cc + 1) * LANES]
                    p = jnp.exp2(su + bias_ref[classes[u], :, sl] + shift)
                    row_sum = p if row_sum is None else row_sum + p
                    ps.append(p.astype(BF16))
            l_ref[m] += row_sum
            acc_ref[m] += jnp.dot(jnp.concatenate(ps, axis=1), vv, preferred_element_type=F32)

    n_steps = hi - lo
    n_wide = lax.shift_right_logical(n_steps, 2)

    def wide_body(i, carry):
        bounded_block(lo + 4 * i, 4)
        return carry

    lax.fori_loop(0, n_wide, wide_body, 0)
    rest = lo + 4 * n_wide

    @pl.when(jnp.bitwise_and(n_steps, 2) != 0)
    def _():
        bounded_block(rest, 2)

    @pl.when(jnp.bitwise_and(n_steps, 1) != 0)
    def _():
        bounded_block(hi - 1, 1)

    def finalize():
        lam = (jnp.exp(jnp.sum(lq1_ref[...] * lk1_ref[...], axis=-1, keepdims=True))
               - jnp.exp(jnp.sum(lq2_ref[...] * lk2_ref[...], axis=-1, keepdims=True))
               + lambda_init)
        o = acc_ref[0] / ls_ref[0] - lam * (acc_ref[1] / ls_ref[1])
        o = _rmsnorm(o, subg_ref[...]) * (1.0 - lambda_init)
        o_ref[0] = o.astype(BF16)

    for m in range(2):
        ls_ref[m] = jnp.sum(l_ref[m], axis=-1, keepdims=True)
    smallest = jnp.min(jnp.minimum(ls_ref[0], ls_ref[1]))
    finalize()

    @pl.when(smallest < MIN_BOUNDED_ROW_SUM)
    def _():
        m_ref[...] = jnp.full(m_ref.shape, MASKED_LOGIT, F32)
        ls_ref[...] = jnp.zeros(ls_ref.shape, F32)
        acc_ref[...] = jnp.zeros(acc_ref.shape, F32)

        def online_step(j, carry):
            start = pl.multiple_of(j * t, t)
            cls = tile_class(j)
            off = tile_offset(j)[:, :1]
            vv = v_ref[0, 0, pl.ds(start, t), :]
            for m in range(2):
                kk = k_ref[0, 0, m, pl.ds(start, t), :]
                s = lax.dot_general(q_ref[0, 0, m], kk, _NT_DIMS, preferred_element_type=F32)
                s = s + bias_ref[cls] + off
                m_prev = m_ref[m]
                m_new = jnp.maximum(m_prev, jnp.max(s, axis=-1, keepdims=True))
                alpha = jnp.exp2(m_prev - m_new)
                p = jnp.exp2(s - m_new)
                ls_ref[m] = alpha * ls_ref[m] + jnp.sum(p, axis=-1, keepdims=True)
                acc_ref[m] = alpha * acc_ref[m] + jnp.dot(p.astype(BF16), vv,
                                                          preferred_element_type=F32)
                m_ref[m] = m_new
            return carry

        lax.fori_loop(lo, hi, online_step, 0)
        finalize()


def _attention_layer(q, k, v, slopes, lq1, lk1, lq2, lk2, subg, lambda_init):
    bsz, n_heads, _, lp, dh = q.shape
    tq = SEQ_TILE
    n_q = lp // tq - 1
    dv = v.shape[-1]
    kv_bytes = 2 * (2 * lp * dh * 2 + lp * dv * 2)
    scratch = (N_BIAS_CLASSES * tq * tq + 4 * tq * LANES + 2 * tq * dv + 4 * tq * LANES) * 4
    blocks = kv_bytes + 2 * 2 * tq * dh * 2 + 2 * tq * dv * 2 + scratch
    vec = lambda b, h, i, *_: (0, 0)
    grid_spec = pltpu.PrefetchScalarGridSpec(
        num_scalar_prefetch=1,
        grid=(bsz, n_heads, n_q),
        in_specs=[
            pl.BlockSpec((1, 1, 2, tq, dh), lambda b, h, i, *_: (b, h, 0, i + 1, 0)),
            pl.BlockSpec((1, 1, 2, lp, dh), lambda b, h, i, *_: (b, h, 0, 0, 0)),
            pl.BlockSpec((1, 1, lp, dv), lambda b, h, i, *_: (b, h, 0, 0)),
            pl.BlockSpec((1, dh), vec), pl.BlockSpec((1, dh), vec),
            pl.BlockSpec((1, dh), vec), pl.BlockSpec((1, dh), vec),
            pl.BlockSpec((1, dv), vec),
        ],
        out_specs=pl.BlockSpec((1, tq, dv), lambda b, h, i, *_: (b, i, h)),
        scratch_shapes=[
            pltpu.VMEM((N_BIAS_CLASSES, tq, tq), F32),
            pltpu.VMEM((2, 1, LANES), F32),
            pltpu.VMEM((2, tq, LANES), F32),
            pltpu.VMEM((2, tq, LANES), F32),
            pltpu.VMEM((2, tq, dv), F32),
            pltpu.VMEM((2, tq, 1), F32),
            pltpu.VMEM((2, tq, 1), F32),
        ],
    )
    return pl.pallas_call(
        functools.partial(_attn_kernel, lambda_init=lambda_init),
        out_shape=jax.ShapeDtypeStruct((bsz, n_q * tq, n_heads * dv), BF16),
        grid_spec=grid_spec,
        compiler_params=pltpu.CompilerParams(
            dimension_semantics=("arbitrary", "arbitrary", "arbitrary"),
            vmem_limit_bytes=_vmem_limit(blocks + 6 * 4 * tq * tq * 4)),
        name="diff_attention",
    )(slopes, q, k, v, lq1.reshape(1, dh), lk1.reshape(1, dh), lq2.reshape(1, dh),
      lk2.reshape(1, dh), subg.reshape(1, dv))


def _encode(x, p):
    bsz, s, d = x.shape
    assert s % SEQ_TILE == 0 and SEQ_TILE == TOKEN_TILE
    lp = s + SEQ_TILE
    assert (bsz * lp) % WIDE_TOKEN_TILE == 0
    tiles_in, tiles_out = lp // TOKEN_TILE, s // TOKEN_TILE
    front = jnp.concatenate([jnp.zeros((FRONT_PAD, d), x.dtype), p["meta_tokens"]], axis=0)

    h = _pool_layer(x, front, p["mixer_norm_g"][0], p["pool_w"][0], p["pool_scale"][0])
    h = h.reshape(bsz * lp, d)
    h = _ffn_layer(h, p["ffn_norm_g"][0], p["w_gate"][0], p["w_up"][0], p["w_down"][0],
                   p["final_norm_g"], n_out=bsz * lp, x_tile_of=lambda i: i,
                   tm=WIDE_TOKEN_TILE)

    lambda_init = 0.8 - 0.6 * math.exp(-0.3 * 1)
    slopes = 2.0 ** (-8.0 * (jnp.arange(N_HEADS, dtype=F32) + 1.0) / N_HEADS)
    q, k, v = _qkv_layer(h, p["mixer_norm_g"][1], p["w_qkv"][0], bsz, lp)
    o = _attention_layer(q, k, v, slopes, p["lambda_q1"][0], p["lambda_k1"][0],
                         p["lambda_q2"][0], p["lambda_k2"][0], p["subln_g"][0], lambda_init)
    y = _ffn_layer(h, p["ffn_norm_g"][1], p["w_gate"][1], p["w_up"][1], p["w_down"][1],
                   p["final_norm_g"], n_out=bsz * s,
                   x_tile_of=lambda i: (i // tiles_out) * tiles_in + i % tiles_out + 1,
                   tm=TOKEN_TILE, attn=o.reshape(bsz * s, d), w_o=p["w_o"][0], final_norm=True)
    return y.reshape(bsz, s, d)


def kernel(x_prompt, x_sample, meta_tokens, mixer_norm_g, pool_w, pool_scale, w_qkv, lambda_q1,
           lambda_k1, lambda_q2, lambda_k2, subln_g, w_o, ffn_norm_g, w_gate, w_up, w_down,
           final_norm_g):
    def per_layer_bf16(w):
        return [w[i].astype(BF16) for i in range(w.shape[0])]

    p = dict(
        meta_tokens=meta_tokens, mixer_norm_g=mixer_norm_g, pool_scale=pool_scale,
        lambda_q1=lambda_q1, lambda_k1=lambda_k1, lambda_q2=lambda_q2, lambda_k2=lambda_k2,
        subln_g=subln_g, ffn_norm_g=ffn_norm_g, final_norm_g=final_norm_g,
        pool_w=per_layer_bf16(pool_w), w_qkv=per_layer_bf16(w_qkv), w_o=per_layer_bf16(w_o),
        w_gate=per_layer_bf16(w_gate), w_up=per_layer_bf16(w_up),
        w_down=per_layer_bf16(w_down),
    )
    return (_encode(x_prompt, p), _encode(x_sample, p))
```

```python
import functools
import math

import jax
import jax.numpy as jnp
from jax import lax
from jax.experimental import pallas as pl
from jax.experimental.pallas import tpu as pltpu

D_MODEL = 2048
N_META = 16
N_POOL_GROUPS = 4
POOL_WINDOWS = (2, 4, 8, 16)
POOL_GROUP_DIM = D_MODEL // N_POOL_GROUPS
N_HEADS = 8
HEAD_DIM = 128
V_HEAD_DIM = 2 * HEAD_DIM
D_FF = 5632
RMS_EPS = 1e-6

F32 = jnp.float32
BF16 = jnp.bfloat16

LANES = 128
V7X_VMEM_BYTES = 64 * 1024 * 1024
V7X_VMEM_COMPILER_RESERVE = 8 * 1024 * 1024

SEQ_TILE = 512
FRONT_PAD = SEQ_TILE - N_META
TOKEN_TILE = 512
WIDE_TOKEN_TILE = 1024
FF_TILE = 512
POOL_HALO = 8

LOG2_E = math.log2(math.e)
QK_LOG2_SCALE = HEAD_DIM ** -0.5 * LOG2_E
SKIP_LOG2_THRESHOLD = 90.0 * LOG2_E
MASKED_LOGIT = -1e30
MIN_BOUNDED_ROW_SUM = 2.0 ** -60
BOUND_SLACK = 1.0 + 2.0 ** -6


def _vmem_limit(block_bytes):
    return min(V7X_VMEM_BYTES - V7X_VMEM_COMPILER_RESERVE,
               int(block_bytes) + V7X_VMEM_COMPILER_RESERVE)


def _rmsnorm(x, g):
    ms = jnp.mean(x * x, axis=-1, keepdims=True)
    return x * lax.rsqrt(ms + RMS_EPS) * g


def _pool_kernel(cur_ref, prev_ref, next_ref, front_ref, g_ref, w_ref, scale_ref, out_ref, *,
                 n_pos):
    i = pl.program_id(1)
    n_tiles = pl.num_programs(1)
    tile = out_ref.shape[1]
    x = jnp.where(i == 0, front_ref[...], cur_ref[0])
    front_tail = front_ref[tile - POOL_HALO:, :]
    halo_lo = jnp.where(i >= 2, prev_ref[0], jnp.where(i == 1, front_tail, 0.0))
    halo_hi = jnp.where(i < n_tiles - 1, next_ref[0], 0.0)
    ext = jnp.concatenate([halo_lo, x, halo_hi], axis=0)
    hn = _rmsnorm(ext, g_ref[...])
    n_ext = ext.shape[0]

    def shifted(a, d):
        return pltpu.roll(a, (-d) % n_ext, axis=0)

    pos = i * tile - FRONT_PAD + lax.broadcasted_iota(jnp.int32, (tile, 1), 0)
    valid = pos >= 0
    for gi, w in enumerate(POOL_WINDOWS):
        cols = slice(gi * POOL_GROUP_DIM, (gi + 1) * POOL_GROUP_DIM)
        e = hn[:, cols]
        s = shifted(e, -1) + e
        reach = 1
        while 2 * reach < w:
            s = shifted(s, -reach) + shifted(s, reach)
            reach *= 2
        win = s[POOL_HALO:POOL_HALO + tile]
        cnt = jnp.minimum(pos + w // 2, n_pos) - jnp.maximum(pos - w // 2, 0)
        cnt = jnp.maximum(cnt, 1).astype(F32)
        pooled = win / cnt - e[POOL_HALO:POOL_HALO + tile]
        y = jnp.dot(pooled.astype(BF16), w_ref[gi], preferred_element_type=F32)
        y = y * scale_ref[:, cols]
        out_ref[0, :, cols] = jnp.where(valid, x[:, cols] + y, 0.0)


def _pool_layer(x, front, g, w_bf16, scale):
    bsz, s, d = x.shape
    tile = SEQ_TILE
    nt = s // tile + 1
    hb = tile // POOL_HALO
    n_halo_blocks = s // POOL_HALO
    blk = tile * d * 4
    return pl.pallas_call(
        functools.partial(_pool_kernel, n_pos=s + N_META),
        out_shape=jax.ShapeDtypeStruct((bsz, nt * tile, d), F32),
        grid=(bsz, nt),
        in_specs=[
            pl.BlockSpec((1, tile, d), lambda b, i: (b, jnp.maximum(i - 1, 0), 0)),
            pl.BlockSpec((1, POOL_HALO, d),
                         lambda b, i: (b, jnp.maximum((i - 1) * hb - 1, 0), 0)),
            pl.BlockSpec((1, POOL_HALO, d),
                         lambda b, i: (b, jnp.minimum(i * hb, n_halo_blocks - 1), 0)),
            pl.BlockSpec((tile, d), lambda b, i: (0, 0)),
            pl.BlockSpec((1, d), lambda b, i: (0, 0)),
            pl.BlockSpec((N_POOL_GROUPS, POOL_GROUP_DIM, POOL_GROUP_DIM), lambda b, i: (0, 0, 0)),
            pl.BlockSpec((1, d), lambda b, i: (0, 0)),
        ],
        out_specs=pl.BlockSpec((1, tile, d), lambda b, i: (b, i, 0)),
        compiler_params=pltpu.CompilerParams(
            dimension_semantics=("parallel", "arbitrary"),
            vmem_limit_bytes=_vmem_limit(6 * blk + 4 * w_bf16.size + 4 * blk)),
        name="pool_mixer",
    )(x, x, x, front, g.reshape(1, d), w_bf16, scale.reshape(1, d))


def _ffn_kernel(*refs, with_attn, final_norm):
    if with_attn:
        x_ref, a_ref, wo_ref, g_ref, wg_ref, wu_ref, wd_ref, gf_ref, o_ref, hn_ref = refs
    else:
        x_ref, g_ref, wg_ref, wu_ref, wd_ref, gf_ref, o_ref, hn_ref = refs
    c = pl.program_id(1)

    def ffn_chunk(hn):
        gate = jnp.dot(hn, wg_ref[...], preferred_element_type=F32)
        up = jnp.dot(hn, wu_ref[...], preferred_element_type=F32)
        act = (gate * jax.nn.sigmoid(gate) * up).astype(BF16)
        return jnp.dot(act, wd_ref[...], preferred_element_type=F32)

    @pl.when(c == 0)
    def _():
        x = x_ref[...]
        if with_attn:
            x = x + jnp.dot(a_ref[...], wo_ref[...], preferred_element_type=F32)
        hn = _rmsnorm(x, g_ref[...]).astype(BF16)
        hn_ref[...] = hn
        o_ref[...] = x + ffn_chunk(hn)

    @pl.when(c > 0)
    def _():
        o_ref[...] += ffn_chunk(hn_ref[...])

    if final_norm:
        @pl.when(c == pl.num_programs(1) - 1)
        def _():
            o_ref[...] = _rmsnorm(o_ref[...], gf_ref[...])


def _ffn_layer(x, g, wg, wu, wd, gf, *, layer, n_out, x_tile_of, tm, attn=None, w_o=None,
               final_norm=False):
    d = x.shape[1]
    tf = FF_TILE
    dff = wg.shape[2]
    with_attn = attn is not None
    blocks = 4 * tm * d * 4 + tm * d * 2 + 2 * 3 * d * tf * 2 + 3 * tm * tf * 4
    in_specs = [pl.BlockSpec((tm, d), lambda i, c: (x_tile_of(i), 0))]
    args = [x]
    if with_attn:
        in_specs += [pl.BlockSpec((tm, d), lambda i, c: (i, 0)),
                     pl.BlockSpec((d, d), lambda i, c: (0, 0), pipeline_mode=pl.Buffered(1))]
        args += [attn, w_o]
        blocks += 2 * tm * d * 2 + d * d * 2
    in_specs += [
        pl.BlockSpec((1, d), lambda i, c: (0, 0)),
        pl.BlockSpec((None, d, tf), lambda i, c: (layer, 0, c)),
        pl.BlockSpec((None, d, tf), lambda i, c: (layer, 0, c)),
        pl.BlockSpec((None, tf, d), lambda i, c: (layer, c, 0)),
        pl.BlockSpec((1, d), lambda i, c: (0, 0)),
    ]
    args += [g.reshape(1, d), wg, wu, wd, gf.reshape(1, d)]
    return pl.pallas_call(
        functools.partial(_ffn_kernel, with_attn=with_attn, final_norm=final_norm),
        out_shape=jax.ShapeDtypeStruct((n_out, d), F32),
        grid=(n_out // tm, dff // tf),
        in_specs=in_specs,
        out_specs=pl.BlockSpec((tm, d), lambda i, c: (i, 0)),
        scratch_shapes=[pltpu.VMEM((tm, d), BF16)],
        compiler_params=pltpu.CompilerParams(
            dimension_semantics=("parallel", "arbitrary"),
            vmem_limit_bytes=_vmem_limit(blocks)),
        name="oproj_swiglu_ffn_final" if with_attn else "swiglu_ffn",
    )(*args)


def _qkv_kernel(x_ref, g_ref, w_ref, q_ref, k_ref, v_ref, sq_ref):
    hn = _rmsnorm(x_ref[...], g_ref[...]).astype(BF16)
    for h in range(N_HEADS):
        for part, ref in ((0, q_ref), (1, k_ref)):
            c0 = part * D_MODEL + h * V_HEAD_DIM
            r = jnp.dot(hn, w_ref[:, c0:c0 + V_HEAD_DIM], preferred_element_type=F32)
            if part == 0:
                r = r * QK_LOG2_SCALE
            for m in range(2):
                rm = r[:, m * HEAD_DIM:(m + 1) * HEAD_DIM]
                ref[0, h, m] = rm.astype(BF16)
                biggest = jnp.max(jnp.sum(rm * rm, axis=-1, keepdims=True), axis=0,
                                  keepdims=True)
                row = (2 * h + part) * 2 + m
                sq_ref[0, row:row + 1, :] = jnp.broadcast_to(biggest, (1, LANES))
        c0 = 2 * D_MODEL + h * V_HEAD_DIM
        v_ref[0, h] = jnp.dot(hn, w_ref[:, c0:c0 + V_HEAD_DIM],
                              preferred_element_type=F32).astype(BF16)


def _qkv_layer(x, g, w_bf16, bsz, lp):
    n, d = x.shape
    tm = TOKEN_TILE
    nt = lp // tm
    qk_shape = jax.ShapeDtypeStruct((bsz, N_HEADS, 2, lp, HEAD_DIM), BF16)
    v_shape = jax.ShapeDtypeStruct((bsz, N_HEADS, lp, V_HEAD_DIM), BF16)
    sq_shape = jax.ShapeDtypeStruct((n // tm, 4 * N_HEADS, LANES), F32)
    blocks = 2 * tm * d * 4 + w_bf16.size * 2 + 2 * 3 * tm * d * 2 + tm * d * 2
    return pl.pallas_call(
        _qkv_kernel,
        out_shape=(qk_shape, qk_shape, v_shape, sq_shape),
        grid=(n // tm,),
        in_specs=[
            pl.BlockSpec((tm, d), lambda i: (i, 0)),
            pl.BlockSpec((1, d), lambda i: (0, 0)),
            pl.BlockSpec((d, 3 * d), lambda i: (0, 0), pipeline_mode=pl.Buffered(1)),
        ],
        out_specs=(
            pl.BlockSpec((1, N_HEADS, 2, tm, HEAD_DIM), lambda i: (i // nt, 0, 0, i % nt, 0)),
            pl.BlockSpec((1, N_HEADS, 2, tm, HEAD_DIM), lambda i: (i // nt, 0, 0, i % nt, 0)),
            pl.BlockSpec((1, N_HEADS, tm, V_HEAD_DIM), lambda i: (i // nt, 0, i % nt, 0)),
            pl.BlockSpec((1, 4 * N_HEADS, LANES), lambda i: (i, 0, 0)),
        ),
        compiler_params=pltpu.CompilerParams(
            dimension_semantics=("parallel",),
            vmem_limit_bytes=_vmem_limit(blocks)),
        name="qkv_proj",
    )(x, g.reshape(1, d), w_bf16)


BIAS_LEFT, BIAS_DIAG, BIAS_RIGHT, BIAS_FRONT = 0, 1, 2, 3
N_BIAS_CLASSES = 4
_NT_DIMS = (((1,), (1,)), ((), ()))


def _attn_kernel(slope_ref, bound_ref, reach_ref, q_ref, k_ref, v_ref, lq1_ref, lk1_ref, lq2_ref,
                 lk2_ref, subg_ref, o_ref, bias_ref, l_ref, acc_ref, ls_ref, m_ref, lrun_ref, *,
                 lambda_init):
    b, h = pl.program_id(0), pl.program_id(1)
    qi = pl.program_id(2) + 1
    t = q_ref.shape[3]
    n_rows = k_ref.shape[3]
    n_chunks = t // LANES
    log2_t = t.bit_length() - 1
    assert t == 1 << log2_t
    sigma = slope_ref[h] * LOG2_E
    bounds = [bound_ref[(b * N_HEADS + h) * 2 + m] for m in range(2)]
    reach = reach_ref[b * N_HEADS + h]
    lo = lax.shift_right_logical(jnp.maximum(qi * t - reach, 0), log2_t)
    hi = lax.shift_right_logical(jnp.minimum(qi * t + (t - 1) + reach, n_rows - 1), log2_t) + 1

    @pl.when(qi == 1)
    def _():
        r = lax.broadcasted_iota(jnp.int32, (t, t), 0)
        c = lax.broadcasted_iota(jnp.int32, (t, t), 1)
        left = -sigma * (r - c).astype(F32)
        bias_ref[BIAS_LEFT] = left
        bias_ref[BIAS_DIAG] = -jnp.abs(left)
        bias_ref[BIAS_RIGHT] = -left
        bias_ref[BIAS_FRONT] = jnp.where(c >= FRONT_PAD, left, MASKED_LOGIT)

    def lane_sums(x):
        ones = jnp.ones((x.shape[1], LANES), BF16)
        hi_part = x.astype(BF16)
        lo_part = (x - hi_part.astype(F32)).astype(BF16)
        return (jnp.dot(hi_part, ones, preferred_element_type=F32)
                + jnp.dot(lo_part, ones, preferred_element_type=F32))

    l_ref[...] = jnp.zeros(l_ref.shape, F32)
    acc_ref[...] = jnp.zeros(acc_ref.shape, F32)

    def tile_class(j):
        side = jnp.where(j < qi, BIAS_LEFT, jnp.where(j == qi, BIAS_DIAG, BIAS_RIGHT))
        return jnp.where(j == 0, BIAS_FRONT, side)

    def tile_offset(j):
        gap = jnp.full((1, LANES), jnp.abs(qi - j) * t, jnp.int32).astype(F32)
        return gap * (-sigma)

    def bounded_block(j0, n_blk):
        start = pl.multiple_of(j0 * t, t)
        width = n_blk * t
        classes = [tile_class(j0 + u) for u in range(n_blk)]
        offsets = [tile_offset(j0 + u) for u in range(n_blk)]
        vv = v_ref[0, 0, pl.ds(start, width), :]
        for m in range(2):
            kk = k_ref[0, 0, m, pl.ds(start, width), :]
            s = lax.dot_general(q_ref[0, 0, m], kk, _NT_DIMS, preferred_element_type=F32)
            row_sum = None
            ps = []
            for u in range(n_blk):
                shift = offsets[u] - bounds[m]
                for cc in range(n_chunks):
                    sl = slice(cc * LANES, (cc + 1) * LANES)
                    su = s[:, u * t + cc * LANES:u * t + (cc + 1) * LANES]
                    p = jnp.exp2(su + bias_ref[classes[u], :, sl] + shift)
                    row_sum = p if row_sum is None else row_sum + p
                    ps.append(p.astype(BF16))
            l_ref[m] += row_sum
            acc_ref[m] += jnp.dot(jnp.concatenate(ps, axis=1), vv, preferred_element_type=F32)

    n_steps = hi - lo
    n_wide = lax.shift_right_logical(n_steps, 2)

    def wide_body(i, carry):
        bounded_block(lo + 4 * i, 4)
        return carry

    lax.fori_loop(0, n_wide, wide_body, 0)
    rest = lo + 4 * n_wide

    @pl.when(jnp.bitwise_and(n_steps, 2) != 0)
    def _():
        bounded_block(rest, 2)

    @pl.when(jnp.bitwise_and(n_steps, 1) != 0)
    def _():
        bounded_block(hi - 1, 1)

    def finalize():
        lam = (jnp.exp(jnp.sum(lq1_ref[...] * lk1_ref[...], axis=-1, keepdims=True))
               - jnp.exp(jnp.sum(lq2_ref[...] * lk2_ref[...], axis=-1, keepdims=True))
               + lambda_init)
        inv = [1.0 / ls_ref[m] for m in range(2)]
        o = (acc_ref[0] * jnp.concatenate([inv[0], inv[0]], axis=1)
             - lam * (acc_ref[1] * jnp.concatenate([inv[1], inv[1]], axis=1)))
        r = lax.rsqrt(lane_sums(o * o) * (1.0 / V_HEAD_DIM) + RMS_EPS)
        o = o * jnp.concatenate([r, r], axis=1) * (subg_ref[...] * (1.0 - lambda_init))
        o_ref[0] = o.astype(BF16)

    for m in range(2):
        ls_ref[m] = lane_sums(l_ref[m])
    smallest = jnp.min(jnp.minimum(ls_ref[0], ls_ref[1]))
    finalize()

    @pl.when(smallest < MIN_BOUNDED_ROW_SUM)
    def _():
        m_ref[...] = jnp.full(m_ref.shape, MASKED_LOGIT, F32)
        lrun_ref[...] = jnp.zeros(lrun_ref.shape, F32)
        acc_ref[...] = jnp.zeros(acc_ref.shape, F32)

        def online_step(j, carry):
            start = pl.multiple_of(j * t, t)
            cls = tile_class(j)
            off = tile_offset(j)[:, :1]
            vv = v_ref[0, 0, pl.ds(start, t), :]
            for m in range(2):
                kk = k_ref[0, 0, m, pl.ds(start, t), :]
                s = lax.dot_general(q_ref[0, 0, m], kk, _NT_DIMS, preferred_element_type=F32)
                s = s + bias_ref[cls] + off
                m_prev = m_ref[m]
                m_new = jnp.maximum(m_prev, jnp.max(s, axis=-1, keepdims=True))
                alpha = jnp.exp2(m_prev - m_new)
                p = jnp.exp2(s - m_new)
                lrun_ref[m] = alpha * lrun_ref[m] + jnp.sum(p, axis=-1, keepdims=True)
                acc_ref[m] = alpha * acc_ref[m] + jnp.dot(p.astype(BF16), vv,
                                                          preferred_element_type=F32)
                m_ref[m] = m_new
            return carry

        lax.fori_loop(lo, hi, online_step, 0)
        for m in range(2):
            ls_ref[m] = jnp.broadcast_to(lrun_ref[m], (t, LANES))
        finalize()


def _attention_bounds(sq, slopes, bsz, lp):
    norms = jnp.sqrt(jnp.max(sq[:, :, 0].reshape(bsz, -1, N_HEADS, 2, 2), axis=1))
    product = norms[:, :, 0] * norms[:, :, 1] * BOUND_SLACK
    spread = 2.0 * jnp.max(product, axis=-1) + 1.0
    reach = (SKIP_LOG2_THRESHOLD + spread) / (slopes * LOG2_E)[None, :]
    reach = jnp.ceil(jnp.minimum(reach, float(lp))).astype(jnp.int32) + 1
    return product.reshape(-1), reach.reshape(-1)


def _attention_layer(q, k, v, slopes, bound, reach, lq1, lk1, lq2, lk2, subg, lambda_init):
    bsz, n_heads, _, lp, dh = q.shape
    tq = SEQ_TILE
    n_q = lp // tq - 1
    dv = v.shape[-1]
    kv_bytes = 2 * (2 * lp * dh * 2 + lp * dv * 2)
    scratch = (N_BIAS_CLASSES * tq * tq + 4 * tq * LANES + 2 * tq * dv) * 4
    blocks = kv_bytes + 2 * 2 * tq * dh * 2 + 2 * tq * dv * 2 + scratch
    vec = lambda b, h, i, *_: (0, 0)
    grid_spec = pltpu.PrefetchScalarGridSpec(
        num_scalar_prefetch=3,
        grid=(bsz, n_heads, n_q),
        in_specs=[
            pl.BlockSpec((1, 1, 2, tq, dh), lambda b, h, i, *_: (b, h, 0, i + 1, 0)),
            pl.BlockSpec((1, 1, 2, lp, dh), lambda b, h, i, *_: (b, h, 0, 0, 0)),
            pl.BlockSpec((1, 1, lp, dv), lambda b, h, i, *_: (b, h, 0, 0)),
            pl.BlockSpec((1, dh), vec), pl.BlockSpec((1, dh), vec),
            pl.BlockSpec((1, dh), vec), pl.BlockSpec((1, dh), vec),
            pl.BlockSpec((1, dv), vec),
        ],
        out_specs=pl.BlockSpec((1, tq, dv), lambda b, h, i, *_: (b, i, h)),
        scratch_shapes=[
            pltpu.VMEM((N_BIAS_CLASSES, tq, tq), F32),
            pltpu.VMEM((2, tq, LANES), F32),
            pltpu.VMEM((2, tq, dv), F32),
            pltpu.VMEM((2, tq, LANES), F32),
            pltpu.VMEM((2, tq, 1), F32),
            pltpu.VMEM((2, tq, 1), F32),
        ],
    )
    return pl.pallas_call(
        functools.partial(_attn_kernel, lambda_init=lambda_init),
        out_shape=jax.ShapeDtypeStruct((bsz, n_q * tq, n_heads * dv), BF16),
        grid_spec=grid_spec,
        compiler_params=pltpu.CompilerParams(
            dimension_semantics=("arbitrary", "arbitrary", "arbitrary"),
            vmem_limit_bytes=_vmem_limit(blocks + 6 * 4 * tq * tq * 4)),
        name="diff_attention",
    )(slopes, bound, reach, q, k, v, lq1.reshape(1, dh), lk1.reshape(1, dh),
      lq2.reshape(1, dh), lk2.reshape(1, dh), subg.reshape(1, dv))


def _encode(x, p):
    bsz, s, d = x.shape
    assert s % SEQ_TILE == 0 and SEQ_TILE == TOKEN_TILE
    lp = s + SEQ_TILE
    assert (bsz * lp) % WIDE_TOKEN_TILE == 0
    tiles_in, tiles_out = lp // TOKEN_TILE, s // TOKEN_TILE
    front = jnp.concatenate([jnp.zeros((FRONT_PAD, d), x.dtype), p["meta_tokens"]], axis=0)

    h = _pool_layer(x, front, p["mixer_norm_g"][0], p["pool_w"][0], p["pool_scale"][0])
    h = h.reshape(bsz * lp, d)
    h = _ffn_layer(h, p["ffn_norm_g"][0], p["w_gate"], p["w_up"], p["w_down"],
                   p["final_norm_g"], layer=0, n_out=bsz * lp, x_tile_of=lambda i: i,
                   tm=WIDE_TOKEN_TILE)

    lambda_init = 0.8 - 0.6 * math.exp(-0.3 * 1)
    slopes = 2.0 ** (-8.0 * (jnp.arange(N_HEADS, dtype=F32) + 1.0) / N_HEADS)
    q, k, v, sq = _qkv_layer(h, p["mixer_norm_g"][1], p["w_qkv"][0], bsz, lp)
    bound, reach = _attention_bounds(sq, slopes, bsz, lp)
    o = _attention_layer(q, k, v, slopes, bound, reach, p["lambda_q1"][0], p["lambda_k1"][0],
                         p["lambda_q2"][0], p["lambda_k2"][0], p["subln_g"][0], lambda_init)
    y = _ffn_layer(h, p["ffn_norm_g"][1], p["w_gate"], p["w_up"], p["w_down"],
                   p["final_norm_g"], layer=1, n_out=bsz * s,
                   x_tile_of=lambda i: (i // tiles_out) * tiles_in + i % tiles_out + 1,
                   tm=TOKEN_TILE, attn=o.reshape(bsz * s, d), w_o=p["w_o"][0], final_norm=True)
    return y.reshape(bsz, s, d)


def kernel(x_prompt, x_sample, meta_tokens, mixer_norm_g, pool_w, pool_scale, w_qkv, lambda_q1,
           lambda_k1, lambda_q2, lambda_k2, subln_g, w_o, ffn_norm_g, w_gate, w_up, w_down,
           final_norm_g):
    p = dict(
        meta_tokens=meta_tokens, mixer_norm_g=mixer_norm_g, pool_scale=pool_scale,
        lambda_q1=lambda_q1, lambda_k1=lambda_k1, lambda_q2=lambda_q2, lambda_k2=lambda_k2,
        subln_g=subln_g, ffn_norm_g=ffn_norm_g, final_norm_g=final_norm_g,
        pool_w=pool_w.astype(BF16), w_qkv=w_qkv.astype(BF16), w_o=w_o.astype(BF16),
        w_gate=w_gate.astype(BF16), w_up=w_up.astype(BF16), w_down=w_down.astype(BF16),
    )
    return (_encode(x_prompt, p), _encode(x_sample, p))
```

```python
import functools
import math

import jax
import jax.numpy as jnp
from jax import lax
from jax.experimental import pallas as pl
from jax.experimental.pallas import tpu as pltpu

D_MODEL = 2048
N_META = 16
N_POOL_GROUPS = 4
POOL_WINDOWS = (2, 4, 8, 16)
POOL_GROUP_DIM = D_MODEL // N_POOL_GROUPS
N_HEADS = 8
HEAD_DIM = 128
V_HEAD_DIM = 2 * HEAD_DIM
D_FF = 5632
RMS_EPS = 1e-6

F32 = jnp.float32
BF16 = jnp.bfloat16

LANES = 128
V7X_VMEM_BYTES = 64 * 1024 * 1024
V7X_VMEM_COMPILER_RESERVE = 8 * 1024 * 1024

SEQ_TILE = 512
FRONT_PAD = SEQ_TILE - N_META
TOKEN_TILE = 512
WIDE_TOKEN_TILE = 1024
FF_TILE = 512
POOL_HALO = 8

LOG2_E = math.log2(math.e)
QK_LOG2_SCALE = HEAD_DIM ** -0.5 * LOG2_E
SKIP_LOG2_THRESHOLD = 90.0 * LOG2_E
MASKED_LOGIT = -1e30
MIN_BOUNDED_ROW_SUM = 2.0 ** -60
BOUND_SLACK = 1.0 + 2.0 ** -6


def _vmem_limit(block_bytes):
    return min(V7X_VMEM_BYTES - V7X_VMEM_COMPILER_RESERVE,
               int(block_bytes) + V7X_VMEM_COMPILER_RESERVE)


def _rmsnorm(x, g):
    ms = jnp.mean(x * x, axis=-1, keepdims=True)
    return x * lax.rsqrt(ms + RMS_EPS) * g


def _pool_kernel(cur_ref, prev_ref, next_ref, front_ref, g_ref, w_ref, scale_ref, out_ref, *,
                 n_pos):
    i = pl.program_id(1)
    n_tiles = pl.num_programs(1)
    tile = out_ref.shape[1]
    x = jnp.where(i == 0, front_ref[...], cur_ref[0])
    front_tail = front_ref[tile - POOL_HALO:, :]
    halo_lo = jnp.where(i >= 2, prev_ref[0], jnp.where(i == 1, front_tail, 0.0))
    halo_hi = jnp.where(i < n_tiles - 1, next_ref[0], 0.0)
    ext = jnp.concatenate([halo_lo, x, halo_hi], axis=0)
    hn = _rmsnorm(ext, g_ref[...])
    n_ext = ext.shape[0]

    def shifted(a, d):
        return pltpu.roll(a, (-d) % n_ext, axis=0)

    pos = i * tile - FRONT_PAD + lax.broadcasted_iota(jnp.int32, (tile, 1), 0)
    valid = pos >= 0
    for gi, w in enumerate(POOL_WINDOWS):
        cols = slice(gi * POOL_GROUP_DIM, (gi + 1) * POOL_GROUP_DIM)
        e = hn[:, cols]
        s = shifted(e, -1) + e
        reach = 1
        while 2 * reach < w:
            s = shifted(s, -reach) + shifted(s, reach)
            reach *= 2
        win = s[POOL_HALO:POOL_HALO + tile]
        cnt = jnp.minimum(pos + w // 2, n_pos) - jnp.maximum(pos - w // 2, 0)
        cnt = jnp.maximum(cnt, 1).astype(F32)
        pooled = win / cnt - e[POOL_HALO:POOL_HALO + tile]
        y = jnp.dot(pooled.astype(BF16), w_ref[gi], preferred_element_type=F32)
        y = y * scale_ref[:, cols]
        out_ref[0, :, cols] = jnp.where(valid, x[:, cols] + y, 0.0)


def _pool_layer(x, front, g, w_bf16, scale):
    bsz, s, d = x.shape
    tile = SEQ_TILE
    nt = s // tile + 1
    hb = tile // POOL_HALO
    n_halo_blocks = s // POOL_HALO
    blk = tile * d * 4
    return pl.pallas_call(
        functools.partial(_pool_kernel, n_pos=s + N_META),
        out_shape=jax.ShapeDtypeStruct((bsz, nt * tile, d), F32),
        grid=(bsz, nt),
        in_specs=[
            pl.BlockSpec((1, tile, d), lambda b, i: (b, jnp.maximum(i - 1, 0), 0)),
            pl.BlockSpec((1, POOL_HALO, d),
                         lambda b, i: (b, jnp.maximum((i - 1) * hb - 1, 0), 0)),
            pl.BlockSpec((1, POOL_HALO, d),
                         lambda b, i: (b, jnp.minimum(i * hb, n_halo_blocks - 1), 0)),
            pl.BlockSpec((tile, d), lambda b, i: (0, 0)),
            pl.BlockSpec((1, d), lambda b, i: (0, 0)),
            pl.BlockSpec((N_POOL_GROUPS, POOL_GROUP_DIM, POOL_GROUP_DIM), lambda b, i: (0, 0, 0)),
            pl.BlockSpec((1, d), lambda b, i: (0, 0)),
        ],
        out_specs=pl.BlockSpec((1, tile, d), lambda b, i: (b, i, 0)),
        compiler_params=pltpu.CompilerParams(
            dimension_semantics=("parallel", "arbitrary"),
            vmem_limit_bytes=_vmem_limit(6 * blk + 4 * w_bf16.size + 4 * blk)),
        name="pool_mixer",
    )(x, x, x, front, g.reshape(1, d), w_bf16, scale.reshape(1, d))


def _ffn_kernel(x_ref, g_ref, wg_ref, wu_ref, wd_ref, gf_ref, o_ref, hn_ref, *, final_norm):
    c = pl.program_id(1)

    def ffn_chunk(hn):
        gate = jnp.dot(hn, wg_ref[...], preferred_element_type=F32)
        up = jnp.dot(hn, wu_ref[...], preferred_element_type=F32)
        act = (gate * jax.nn.sigmoid(gate) * up).astype(BF16)
        return jnp.dot(act, wd_ref[...], preferred_element_type=F32)

    @pl.when(c == 0)
    def _():
        x = x_ref[...]
        hn = _rmsnorm(x, g_ref[...]).astype(BF16)
        hn_ref[...] = hn
        o_ref[...] = x + ffn_chunk(hn)

    @pl.when(c > 0)
    def _():
        o_ref[...] += ffn_chunk(hn_ref[...])

    if final_norm:
        @pl.when(c == pl.num_programs(1) - 1)
        def _():
            o_ref[...] = _rmsnorm(o_ref[...], gf_ref[...])


def _ffn_layer(x, g, wg, wu, wd, gf, *, layer, final_norm):
    n, d = x.shape
    tm, tf = WIDE_TOKEN_TILE, FF_TILE
    dff = wg.shape[2]
    blocks = 4 * tm * d * 4 + tm * d * 2 + 2 * 3 * d * tf * 2 + 3 * tm * tf * 4
    return pl.pallas_call(
        functools.partial(_ffn_kernel, final_norm=final_norm),
        out_shape=jax.ShapeDtypeStruct((n, d), F32),
        grid=(n // tm, dff // tf),
        in_specs=[
            pl.BlockSpec((tm, d), lambda i, c: (i, 0)),
            pl.BlockSpec((1, d), lambda i, c: (0, 0)),
            pl.BlockSpec((None, d, tf), lambda i, c: (layer, 0, c)),
            pl.BlockSpec((None, d, tf), lambda i, c: (layer, 0, c)),
            pl.BlockSpec((None, tf, d), lambda i, c: (layer, c, 0)),
            pl.BlockSpec((1, d), lambda i, c: (0, 0)),
        ],
        out_specs=pl.BlockSpec((tm, d), lambda i, c: (i, 0)),
        scratch_shapes=[pltpu.VMEM((tm, d), BF16)],
        compiler_params=pltpu.CompilerParams(
            dimension_semantics=("parallel", "arbitrary"),
            vmem_limit_bytes=_vmem_limit(blocks)),
        name="swiglu_ffn_final" if final_norm else "swiglu_ffn",
    )(x, g.reshape(1, d), wg, wu, wd, gf.reshape(1, d))


N_STAT_ROWS = 6 * N_HEADS


def _qkv_kernel(x_ref, g_ref, w_ref, q_ref, k_ref, v_ref, st_ref):
    hn = _rmsnorm(x_ref[...], g_ref[...]).astype(BF16)

    def put(row, value):
        st_ref[0, row:row + 1, :] = jnp.broadcast_to(value, (1, LANES))

    for h in range(N_HEADS):
        qk = []
        for part, ref in ((0, q_ref), (1, k_ref)):
            c0 = part * D_MODEL + h * V_HEAD_DIM
            r = jnp.dot(hn, w_ref[:, c0:c0 + V_HEAD_DIM], preferred_element_type=F32)
            if part == 0:
                r = r * QK_LOG2_SCALE
            qk.append(r)
            for m in range(2):
                rm = r[:, m * HEAD_DIM:(m + 1) * HEAD_DIM]
                ref[0, h, m] = rm.astype(BF16)
                put((2 * h + part) * 2 + m,
                    jnp.max(jnp.sum(rm * rm, axis=-1, keepdims=True), axis=0, keepdims=True))
        for m in range(2):
            sl = slice(m * HEAD_DIM, (m + 1) * HEAD_DIM)
            self_logit = jnp.sum(qk[0][:, sl] * qk[1][:, sl], axis=-1, keepdims=True)
            put(4 * N_HEADS + 2 * h + m, jnp.min(self_logit, axis=0, keepdims=True))
        c0 = 2 * D_MODEL + h * V_HEAD_DIM
        v_ref[0, h] = jnp.dot(hn, w_ref[:, c0:c0 + V_HEAD_DIM],
                              preferred_element_type=F32).astype(BF16)


def _qkv_layer(x, g, w_bf16, bsz, lp):
    n, d = x.shape
    tm = TOKEN_TILE
    nt = lp // tm
    qk_shape = jax.ShapeDtypeStruct((bsz, N_HEADS, 2, lp, HEAD_DIM), BF16)
    v_shape = jax.ShapeDtypeStruct((bsz, N_HEADS, lp, V_HEAD_DIM), BF16)
    sq_shape = jax.ShapeDtypeStruct((n // tm, N_STAT_ROWS, LANES), F32)
    blocks = 2 * tm * d * 4 + w_bf16.size * 2 + 2 * 3 * tm * d * 2 + tm * d * 2
    return pl.pallas_call(
        _qkv_kernel,
        out_shape=(qk_shape, qk_shape, v_shape, sq_shape),
        grid=(n // tm,),
        in_specs=[
            pl.BlockSpec((tm, d), lambda i: (i, 0)),
            pl.BlockSpec((1, d), lambda i: (0, 0)),
            pl.BlockSpec((d, 3 * d), lambda i: (0, 0), pipeline_mode=pl.Buffered(1)),
        ],
        out_specs=(
            pl.BlockSpec((1, N_HEADS, 2, tm, HEAD_DIM), lambda i: (i // nt, 0, 0, i % nt, 0)),
            pl.BlockSpec((1, N_HEADS, 2, tm, HEAD_DIM), lambda i: (i // nt, 0, 0, i % nt, 0)),
            pl.BlockSpec((1, N_HEADS, tm, V_HEAD_DIM), lambda i: (i // nt, 0, i % nt, 0)),
            pl.BlockSpec((1, N_STAT_ROWS, LANES), lambda i: (i, 0, 0)),
        ),
        compiler_params=pltpu.CompilerParams(
            dimension_semantics=("parallel",),
            vmem_limit_bytes=_vmem_limit(blocks)),
        name="qkv_proj",
    )(x, g.reshape(1, d), w_bf16)


BIAS_LEFT, BIAS_DIAG, BIAS_RIGHT, BIAS_FRONT = 0, 1, 2, 3
N_BIAS_CLASSES = 4
_NT_DIMS = (((1,), (1,)), ((), ()))


def _attn_kernel(slope_ref, bound_ref, reach_ref, q_ref, k_ref, v_ref, lq1_ref, lk1_ref, lq2_ref,
                 lk2_ref, subg_ref, o_ref, bias_ref, l_ref, acc_ref, ls_ref, m_ref, lrun_ref, *,
                 lambda_init):
    b, h = pl.program_id(0), pl.program_id(1)
    qi = pl.program_id(2) + 1
    t = q_ref.shape[3]
    n_rows = k_ref.shape[3]
    n_chunks = t // LANES
    log2_t = t.bit_length() - 1
    assert t == 1 << log2_t
    sigma = slope_ref[h] * LOG2_E
    bounds = [bound_ref[(b * N_HEADS + h) * 2 + m] for m in range(2)]
    reach = reach_ref[b * N_HEADS + h]
    lo = lax.shift_right_logical(jnp.maximum(qi * t - reach, 0), log2_t)
    hi = lax.shift_right_logical(jnp.minimum(qi * t + (t - 1) + reach, n_rows - 1), log2_t) + 1

    @pl.when(qi == 1)
    def _():
        r = lax.broadcasted_iota(jnp.int32, (t, t), 0)
        c = lax.broadcasted_iota(jnp.int32, (t, t), 1)
        left = -sigma * (r - c).astype(F32)
        bias_ref[BIAS_LEFT] = left
        bias_ref[BIAS_DIAG] = -jnp.abs(left)
        bias_ref[BIAS_RIGHT] = -left
        bias_ref[BIAS_FRONT] = jnp.where(c >= FRONT_PAD, left, MASKED_LOGIT)

    def lane_sums(x):
        ones = jnp.ones((x.shape[1], LANES), BF16)
        hi_part = x.astype(BF16)
        lo_part = (x - hi_part.astype(F32)).astype(BF16)
        return (jnp.dot(hi_part, ones, preferred_element_type=F32)
                + jnp.dot(lo_part, ones, preferred_element_type=F32))

    l_ref[...] = jnp.zeros(l_ref.shape, F32)
    acc_ref[...] = jnp.zeros(acc_ref.shape, F32)

    def tile_class(j):
        side = jnp.where(j < qi, BIAS_LEFT, jnp.where(j == qi, BIAS_DIAG, BIAS_RIGHT))
        return jnp.where(j == 0, BIAS_FRONT, side)

    def tile_offset(j):
        gap = jnp.full((1, LANES), jnp.abs(qi - j) * t, jnp.int32).astype(F32)
        return gap * (-sigma)

    def bounded_block(j0, n_blk):
        start = pl.multiple_of(j0 * t, t)
        width = n_blk * t
        classes = [tile_class(j0 + u) for u in range(n_blk)]
        offsets = [tile_offset(j0 + u) for u in range(n_blk)]
        vv = v_ref[0, 0, pl.ds(start, width), :]
        for m in range(2):
            kk = k_ref[0, 0, m, pl.ds(start, width), :]
            s = lax.dot_general(q_ref[0, 0, m], kk, _NT_DIMS, preferred_element_type=F32)
            row_sum = None
            ps = []
            for u in range(n_blk):
                shift = offsets[u] - bounds[m]
                for cc in range(n_chunks):
                    sl = slice(cc * LANES, (cc + 1) * LANES)
                    su = s[:, u * t + cc * LANES:u * t + (cc + 1) * LANES]
                    p = jnp.exp2(su + bias_ref[classes[u], :, sl] + shift)
                    row_sum = p if row_sum is None else row_sum + p
                    ps.append(p.astype(BF16))
            l_ref[m] += row_sum
            acc_ref[m] += jnp.dot(jnp.concatenate(ps, axis=1), vv, preferred_element_type=F32)

    n_steps = hi - lo
    n_wide = lax.shift_right_logical(n_steps, 2)

    def wide_body(i, carry):
        bounded_block(lo + 4 * i, 4)
        return carry

    lax.fori_loop(0, n_wide, wide_body, 0)
    rest = lo + 4 * n_wide

    @pl.when(jnp.bitwise_and(n_steps, 2) != 0)
    def _():
        bounded_block(rest, 2)

    @pl.when(jnp.bitwise_and(n_steps, 1) != 0)
    def _():
        bounded_block(hi - 1, 1)

    def finalize():
        lam = (jnp.exp(jnp.sum(lq1_ref[...] * lk1_ref[...], axis=-1, keepdims=True))
               - jnp.exp(jnp.sum(lq2_ref[...] * lk2_ref[...], axis=-1, keepdims=True))
               + lambda_init)
        inv = [1.0 / ls_ref[m] for m in range(2)]
        o = (acc_ref[0] * jnp.concatenate([inv[0], inv[0]], axis=1)
             - lam * (acc_ref[1] * jnp.concatenate([inv[1], inv[1]], axis=1)))
        r = lax.rsqrt(lane_sums(o * o) * (1.0 / V_HEAD_DIM) + RMS_EPS)
        o = o * jnp.concatenate([r, r], axis=1) * (subg_ref[...] * (1.0 - lambda_init))
        o_ref[0] = o.astype(BF16)

    for m in range(2):
        ls_ref[m] = lane_sums(l_ref[m])
    smallest = jnp.min(jnp.minimum(ls_ref[0], ls_ref[1]))
    finalize()

    @pl.when(smallest < MIN_BOUNDED_ROW_SUM)
    def _():
        m_ref[...] = jnp.full(m_ref.shape, MASKED_LOGIT, F32)
        lrun_ref[...] = jnp.zeros(lrun_ref.shape, F32)
        acc_ref[...] = jnp.zeros(acc_ref.shape, F32)

        def online_step(j, carry):
            start = pl.multiple_of(j * t, t)
            cls = tile_class(j)
            off = tile_offset(j)[:, :1]
            vv = v_ref[0, 0, pl.ds(start, t), :]
            for m in range(2):
                kk = k_ref[0, 0, m, pl.ds(start, t), :]
                s = lax.dot_general(q_ref[0, 0, m], kk, _NT_DIMS, preferred_element_type=F32)
                s = s + bias_ref[cls] + off
                m_prev = m_ref[m]
                m_new = jnp.maximum(m_prev, jnp.max(s, axis=-1, keepdims=True))
                alpha = jnp.exp2(m_prev - m_new)
                p = jnp.exp2(s - m_new)
                lrun_ref[m] = alpha * lrun_ref[m] + jnp.sum(p, axis=-1, keepdims=True)
                acc_ref[m] = alpha * acc_ref[m] + jnp.dot(p.astype(BF16), vv,
                                                          preferred_element_type=F32)
                m_ref[m] = m_new
            return carry

        lax.fori_loop(lo, hi, online_step, 0)
        for m in range(2):
            ls_ref[m] = jnp.broadcast_to(lrun_ref[m], (t, LANES))
        finalize()


def _attention_bounds(stats, slopes, bsz, lp):
    stats = stats[:, :, 0].reshape(bsz, -1, N_STAT_ROWS)
    norms = jnp.sqrt(jnp.max(stats[:, :, :4 * N_HEADS], axis=1)).reshape(bsz, N_HEADS, 2, 2)
    lowest_self = jnp.min(stats[:, :, 4 * N_HEADS:], axis=1).reshape(bsz, N_HEADS, 2)
    product = norms[:, :, 0] * norms[:, :, 1] * BOUND_SLACK
    spread = jnp.max(product * BOUND_SLACK - lowest_self, axis=-1) + 1.0
    reach = (SKIP_LOG2_THRESHOLD + spread) / (slopes * LOG2_E)[None, :]
    reach = jnp.ceil(jnp.minimum(reach, float(lp))).astype(jnp.int32) + 1
    return product.reshape(-1), reach.reshape(-1)


def _attention_layer(q, k, v, slopes, bound, reach, lq1, lk1, lq2, lk2, subg, lambda_init):
    bsz, n_heads, _, lp, dh = q.shape
    tq = SEQ_TILE
    n_q = lp // tq - 1
    dv = v.shape[-1]
    kv_bytes = 2 * (2 * lp * dh * 2 + lp * dv * 2)
    scratch = (N_BIAS_CLASSES * tq * tq + 4 * tq * LANES + 2 * tq * dv) * 4
    blocks = kv_bytes + 2 * 2 * tq * dh * 2 + 2 * tq * dv * 2 + scratch
    vec = lambda b, h, i, *_: (0, 0)
    grid_spec = pltpu.PrefetchScalarGridSpec(
        num_scalar_prefetch=3,
        grid=(bsz, n_heads, n_q),
        in_specs=[
            pl.BlockSpec((1, 1, 2, tq, dh), lambda b, h, i, *_: (b, h, 0, i + 1, 0)),
            pl.BlockSpec((1, 1, 2, lp, dh), lambda b, h, i, *_: (b, h, 0, 0, 0)),
            pl.BlockSpec((1, 1, lp, dv), lambda b, h, i, *_: (b, h, 0, 0)),
            pl.BlockSpec((1, dh), vec), pl.BlockSpec((1, dh), vec),
            pl.BlockSpec((1, dh), vec), pl.BlockSpec((1, dh), vec),
            pl.BlockSpec((1, dv), vec),
        ],
        out_specs=pl.BlockSpec((1, tq, dv), lambda b, h, i, *_: (b, i, h)),
        scratch_shapes=[
            pltpu.VMEM((N_BIAS_CLASSES, tq, tq), F32),
            pltpu.VMEM((2, tq, LANES), F32),
            pltpu.VMEM((2, tq, dv), F32),
            pltpu.VMEM((2, tq, LANES), F32),
            pltpu.VMEM((2, tq, 1), F32),
            pltpu.VMEM((2, tq, 1), F32),
        ],
    )
    return pl.pallas_call(
        functools.partial(_attn_kernel, lambda_init=lambda_init),
        out_shape=jax.ShapeDtypeStruct((bsz, n_q * tq, n_heads * dv), BF16),
        grid_spec=grid_spec,
        compiler_params=pltpu.CompilerParams(
            dimension_semantics=("arbitrary", "arbitrary", "arbitrary"),
            vmem_limit_bytes=_vmem_limit(blocks + 6 * 4 * tq * tq * 4)),
        name="diff_attention",
    )(slopes, bound, reach, q, k, v, lq1.reshape(1, dh), lk1.reshape(1, dh),
      lq2.reshape(1, dh), lk2.reshape(1, dh), subg.reshape(1, dv))


def _oproj_kernel(h_ref, a_ref, w_ref, out_ref):
    out_ref[...] = h_ref[...] + jnp.dot(a_ref[...], w_ref[...], preferred_element_type=F32)


def _oproj_layer(h, attn, w_bf16, h_tile_of):
    n, d = attn.shape
    tm = TOKEN_TILE
    blocks = 4 * tm * d * 4 + 2 * tm * d * 2 + 2 * d * d * 2 + tm * d * 4
    return pl.pallas_call(
        _oproj_kernel,
        out_shape=jax.ShapeDtypeStruct((n, d), F32),
        grid=(n // tm,),
        in_specs=[
            pl.BlockSpec((tm, d), lambda i: (h_tile_of(i), 0)),
            pl.BlockSpec((tm, d), lambda i: (i, 0)),
            pl.BlockSpec((d, d), lambda i: (0, 0)),
        ],
        out_specs=pl.BlockSpec((tm, d), lambda i: (i, 0)),
        compiler_params=pltpu.CompilerParams(
            dimension_semantics=("parallel",),
            vmem_limit_bytes=_vmem_limit(blocks)),
        name="attn_out_proj",
    )(h, attn, w_bf16)


def _encode(x, p):
    bsz, s, d = x.shape
    assert s % SEQ_TILE == 0 and SEQ_TILE == TOKEN_TILE
    lp = s + SEQ_TILE
    assert (bsz * lp) % WIDE_TOKEN_TILE == 0 and (bsz * s) % WIDE_TOKEN_TILE == 0
    tiles_in, tiles_out = lp // TOKEN_TILE, s // TOKEN_TILE
    front = jnp.concatenate([jnp.zeros((FRONT_PAD, d), x.dtype), p["meta_tokens"]], axis=0)

    h = _pool_layer(x, front, p["mixer_norm_g"][0], p["pool_w"][0], p["pool_scale"][0])
    h = h.reshape(bsz * lp, d)
    h = _ffn_layer(h, p["ffn_norm_g"][0], p["w_gate"], p["w_up"], p["w_down"],
                   p["final_norm_g"], layer=0, final_norm=False)

    lambda_init = 0.8 - 0.6 * math.exp(-0.3 * 1)
    slopes = 2.0 ** (-8.0 * (jnp.arange(N_HEADS, dtype=F32) + 1.0) / N_HEADS)
    q, k, v, sq = _qkv_layer(h, p["mixer_norm_g"][1], p["w_qkv"][0], bsz, lp)
    bound, reach = _attention_bounds(sq, slopes, bsz, lp)
    o = _attention_layer(q, k, v, slopes, bound, reach, p["lambda_q1"][0], p["lambda_k1"][0],
                         p["lambda_q2"][0], p["lambda_k2"][0], p["subln_g"][0], lambda_init)
    h = _oproj_layer(h, o.reshape(bsz * s, d), p["w_o"][0],
                     h_tile_of=lambda i: (i // tiles_out) * tiles_in + i % tiles_out + 1)
    y = _ffn_layer(h, p["ffn_norm_g"][1], p["w_gate"], p["w_up"], p["w_down"],
                   p["final_norm_g"], layer=1, final_norm=True)
    return y.reshape(bsz, s, d)


def kernel(x_prompt, x_sample, meta_tokens, mixer_norm_g, pool_w, pool_scale, w_qkv, lambda_q1,
           lambda_k1, lambda_q2, lambda_k2, subln_g, w_o, ffn_norm_g, w_gate, w_up, w_down,
           final_norm_g):
    p = dict(
        meta_tokens=meta_tokens, mixer_norm_g=mixer_norm_g, pool_scale=pool_scale,
        lambda_q1=lambda_q1, lambda_k1=lambda_k1, lambda_q2=lambda_q2, lambda_k2=lambda_k2,
        subln_g=subln_g, ffn_norm_g=ffn_norm_g, final_norm_g=final_norm_g,
        pool_w=pool_w.astype(BF16), w_qkv=w_qkv.astype(BF16), w_o=w_o.astype(BF16),
        w_gate=w_gate.astype(BF16), w_up=w_up.astype(BF16), w_down=w_down.astype(BF16),
    )
    return (_encode(x_prompt, p), _encode(x_sample, p))
```

```python
import functools
import math

import jax
import jax.numpy as jnp
from jax import lax
from jax.experimental import pallas as pl
from jax.experimental.pallas import tpu as pltpu

D_MODEL = 2048
N_META = 16
N_POOL_GROUPS = 4
POOL_WINDOWS = (2, 4, 8, 16)
POOL_GROUP_DIM = D_MODEL // N_POOL_GROUPS
N_HEADS = 8
HEAD_DIM = 128
V_HEAD_DIM = 2 * HEAD_DIM
D_FF = 5632
RMS_EPS = 1e-6

F32 = jnp.float32
BF16 = jnp.bfloat16

LANES = 128
V7X_VMEM_BYTES = 64 * 1024 * 1024
V7X_VMEM_COMPILER_RESERVE = 8 * 1024 * 1024

SEQ_TILE = 512
FRONT_PAD = SEQ_TILE - N_META
TOKEN_TILE = 512
WIDE_TOKEN_TILE = 1024
FF_TILE = 512
POOL_HALO = 8
WIDEST_KEY_BLOCK = 8

LOG2_E = math.log2(math.e)
QK_LOG2_SCALE = HEAD_DIM ** -0.5 * LOG2_E
SKIP_LOG2_THRESHOLD = 90.0 * LOG2_E
MASKED_LOGIT = -1e30
MIN_BOUNDED_ROW_SUM = 2.0 ** -60
BOUND_SLACK = 1.0 + 2.0 ** -6


def _vmem_limit(block_bytes):
    return min(V7X_VMEM_BYTES - V7X_VMEM_COMPILER_RESERVE,
               int(block_bytes) + V7X_VMEM_COMPILER_RESERVE)


def _rmsnorm(x, g):
    ms = jnp.mean(x * x, axis=-1, keepdims=True)
    return x * lax.rsqrt(ms + RMS_EPS) * g


def _pool_kernel(cur_ref, prev_ref, next_ref, front_ref, g_ref, w_ref, scale_ref, out_ref, *,
                 n_pos):
    i = pl.program_id(1)
    n_tiles = pl.num_programs(1)
    tile = out_ref.shape[1]
    x = jnp.where(i == 0, front_ref[...], cur_ref[0])
    front_tail = front_ref[tile - POOL_HALO:, :]
    halo_lo = jnp.where(i >= 2, prev_ref[0], jnp.where(i == 1, front_tail, 0.0))
    halo_hi = jnp.where(i < n_tiles - 1, next_ref[0], 0.0)
    ext = jnp.concatenate([halo_lo, x, halo_hi], axis=0)
    hn = _rmsnorm(ext, g_ref[...])
    n_ext = ext.shape[0]

    def shifted(a, d):
        return pltpu.roll(a, (-d) % n_ext, axis=0)

    pos = i * tile - FRONT_PAD + lax.broadcasted_iota(jnp.int32, (tile, 1), 0)
    valid = pos >= 0
    for gi, w in enumerate(POOL_WINDOWS):
        cols = slice(gi * POOL_GROUP_DIM, (gi + 1) * POOL_GROUP_DIM)
        e = hn[:, cols]
        s = shifted(e, -1) + e
        reach = 1
        while 2 * reach < w:
            s = shifted(s, -reach) + shifted(s, reach)
            reach *= 2
        win = s[POOL_HALO:POOL_HALO + tile]
        cnt = jnp.minimum(pos + w // 2, n_pos) - jnp.maximum(pos - w // 2, 0)
        cnt = jnp.maximum(cnt, 1).astype(F32)
        pooled = win / cnt - e[POOL_HALO:POOL_HALO + tile]
        y = jnp.dot(pooled.astype(BF16), w_ref[gi], preferred_element_type=F32)
        y = y * scale_ref[:, cols]
        out_ref[0, :, cols] = jnp.where(valid, x[:, cols] + y, 0.0)


def _pool_layer(x, front, g, w_bf16, scale):
    bsz, s, d = x.shape
    tile = SEQ_TILE
    nt = s // tile + 1
    hb = tile // POOL_HALO
    n_halo_blocks = s // POOL_HALO
    blk = tile * d * 4
    return pl.pallas_call(
        functools.partial(_pool_kernel, n_pos=s + N_META),
        out_shape=jax.ShapeDtypeStruct((bsz, nt * tile, d), F32),
        grid=(bsz, nt),
        in_specs=[
            pl.BlockSpec((1, tile, d), lambda b, i: (b, jnp.maximum(i - 1, 0), 0)),
            pl.BlockSpec((1, POOL_HALO, d),
                         lambda b, i: (b, jnp.maximum((i - 1) * hb - 1, 0), 0)),
            pl.BlockSpec((1, POOL_HALO, d),
                         lambda b, i: (b, jnp.minimum(i * hb, n_halo_blocks - 1), 0)),
            pl.BlockSpec((tile, d), lambda b, i: (0, 0)),
            pl.BlockSpec((1, d), lambda b, i: (0, 0)),
            pl.BlockSpec((N_POOL_GROUPS, POOL_GROUP_DIM, POOL_GROUP_DIM), lambda b, i: (0, 0, 0)),
            pl.BlockSpec((1, d), lambda b, i: (0, 0)),
        ],
        out_specs=pl.BlockSpec((1, tile, d), lambda b, i: (b, i, 0)),
        compiler_params=pltpu.CompilerParams(
            dimension_semantics=("parallel", "arbitrary"),
            vmem_limit_bytes=_vmem_limit(6 * blk + 4 * w_bf16.size + 4 * blk)),
        name="pool_mixer",
    )(x, x, x, front, g.reshape(1, d), w_bf16, scale.reshape(1, d))


def _ffn_kernel(x_ref, g_ref, wg_ref, wu_ref, wd_ref, gf_ref, o_ref, hn_ref, *, final_norm):
    c = pl.program_id(1)

    def ffn_chunk(hn):
        gate = jnp.dot(hn, wg_ref[...], preferred_element_type=F32)
        up = jnp.dot(hn, wu_ref[...], preferred_element_type=F32)
        act = (gate * jax.nn.sigmoid(gate) * up).astype(BF16)
        return jnp.dot(act, wd_ref[...], preferred_element_type=F32)

    @pl.when(c == 0)
    def _():
        x = x_ref[...]
        hn = _rmsnorm(x, g_ref[...]).astype(BF16)
        hn_ref[...] = hn
        o_ref[...] = x + ffn_chunk(hn)

    @pl.when(c > 0)
    def _():
        o_ref[...] += ffn_chunk(hn_ref[...])

    if final_norm:
        @pl.when(c == pl.num_programs(1) - 1)
        def _():
            o_ref[...] = _rmsnorm(o_ref[...], gf_ref[...])


def _ffn_layer(x, g, wg, wu, wd, gf, *, layer, final_norm):
    n, d = x.shape
    tm, tf = WIDE_TOKEN_TILE, FF_TILE
    dff = wg.shape[2]
    blocks = 4 * tm * d * 4 + tm * d * 2 + 2 * 3 * d * tf * 2 + 3 * tm * tf * 4
    return pl.pallas_call(
        functools.partial(_ffn_kernel, final_norm=final_norm),
        out_shape=jax.ShapeDtypeStruct((n, d), F32),
        grid=(n // tm, dff // tf),
        in_specs=[
            pl.BlockSpec((tm, d), lambda i, c: (i, 0)),
            pl.BlockSpec((1, d), lambda i, c: (0, 0)),
            pl.BlockSpec((None, d, tf), lambda i, c: (layer, 0, c)),
            pl.BlockSpec((None, d, tf), lambda i, c: (layer, 0, c)),
            pl.BlockSpec((None, tf, d), lambda i, c: (layer, c, 0)),
            pl.BlockSpec((1, d), lambda i, c: (0, 0)),
        ],
        out_specs=pl.BlockSpec((tm, d), lambda i, c: (i, 0)),
        scratch_shapes=[pltpu.VMEM((tm, d), BF16)],
        compiler_params=pltpu.CompilerParams(
            dimension_semantics=("parallel", "arbitrary"),
            vmem_limit_bytes=_vmem_limit(blocks)),
        name="swiglu_ffn_final" if final_norm else "swiglu_ffn",
    )(x, g.reshape(1, d), wg, wu, wd, gf.reshape(1, d))


N_STAT_ROWS = 6 * N_HEADS


def _qkv_kernel(x_ref, g_ref, w_ref, q_ref, k_ref, v_ref, st_ref):
    hn = _rmsnorm(x_ref[...], g_ref[...]).astype(BF16)

    def put(row, value):
        st_ref[0, row:row + 1, :] = jnp.broadcast_to(value, (1, LANES))

    for h in range(N_HEADS):
        qk = []
        for part, ref in ((0, q_ref), (1, k_ref)):
            c0 = part * D_MODEL + h * V_HEAD_DIM
            r = jnp.dot(hn, w_ref[:, c0:c0 + V_HEAD_DIM], preferred_element_type=F32)
            if part == 0:
                r = r * QK_LOG2_SCALE
            qk.append(r)
            for m in range(2):
                rm = r[:, m * HEAD_DIM:(m + 1) * HEAD_DIM]
                ref[0, h, m] = rm.astype(BF16)
                put((2 * h + part) * 2 + m,
                    jnp.max(jnp.sum(rm * rm, axis=-1, keepdims=True), axis=0, keepdims=True))
        for m in range(2):
            sl = slice(m * HEAD_DIM, (m + 1) * HEAD_DIM)
            self_logit = jnp.sum(qk[0][:, sl] * qk[1][:, sl], axis=-1, keepdims=True)
            put(4 * N_HEADS + 2 * h + m, jnp.min(self_logit, axis=0, keepdims=True))
        c0 = 2 * D_MODEL + h * V_HEAD_DIM
        v_ref[0, h] = jnp.dot(hn, w_ref[:, c0:c0 + V_HEAD_DIM],
                              preferred_element_type=F32).astype(BF16)


def _qkv_layer(x, g, w_bf16, bsz, lp):
    n, d = x.shape
    tm = TOKEN_TILE
    nt = lp // tm
    qk_shape = jax.ShapeDtypeStruct((bsz, N_HEADS, 2, lp, HEAD_DIM), BF16)
    v_shape = jax.ShapeDtypeStruct((bsz, N_HEADS, lp, V_HEAD_DIM), BF16)
    sq_shape = jax.ShapeDtypeStruct((n // tm, N_STAT_ROWS, LANES), F32)
    blocks = 2 * tm * d * 4 + w_bf16.size * 2 + 2 * 3 * tm * d * 2 + tm * d * 2
    return pl.pallas_call(
        _qkv_kernel,
        out_shape=(qk_shape, qk_shape, v_shape, sq_shape),
        grid=(n // tm,),
        in_specs=[
            pl.BlockSpec((tm, d), lambda i: (i, 0)),
            pl.BlockSpec((1, d), lambda i: (0, 0)),
            pl.BlockSpec((d, 3 * d), lambda i: (0, 0), pipeline_mode=pl.Buffered(1)),
        ],
        out_specs=(
            pl.BlockSpec((1, N_HEADS, 2, tm, HEAD_DIM), lambda i: (i // nt, 0, 0, i % nt, 0)),
            pl.BlockSpec((1, N_HEADS, 2, tm, HEAD_DIM), lambda i: (i // nt, 0, 0, i % nt, 0)),
            pl.BlockSpec((1, N_HEADS, tm, V_HEAD_DIM), lambda i: (i // nt, 0, i % nt, 0)),
            pl.BlockSpec((1, N_STAT_ROWS, LANES), lambda i: (i, 0, 0)),
        ),
        compiler_params=pltpu.CompilerParams(
            dimension_semantics=("parallel",),
            vmem_limit_bytes=_vmem_limit(blocks)),
        name="qkv_proj",
    )(x, g.reshape(1, d), w_bf16)


BIAS_LEFT, BIAS_DIAG, BIAS_RIGHT, BIAS_FRONT = 0, 1, 2, 3
N_BIAS_CLASSES = 4
_NT_DIMS = (((1,), (1,)), ((), ()))


def _attn_kernel(slope_ref, bound_ref, reach_ref, q_ref, k_ref, v_ref, lq1_ref, lk1_ref, lq2_ref,
                 lk2_ref, subg_ref, o_ref, bias_ref, l_ref, acc_ref, ls_ref, m_ref, lrun_ref, *,
                 lambda_init):
    b, h = pl.program_id(0), pl.program_id(1)
    qi = pl.program_id(2) + 1
    t = q_ref.shape[3]
    n_rows = k_ref.shape[3]
    n_chunks = t // LANES
    log2_t = t.bit_length() - 1
    assert t == 1 << log2_t
    sigma = slope_ref[h] * LOG2_E
    bounds = [bound_ref[(b * N_HEADS + h) * 2 + m] for m in range(2)]
    reach = reach_ref[b * N_HEADS + h]
    lo = lax.shift_right_logical(jnp.maximum(qi * t - reach, 0), log2_t)
    hi = lax.shift_right_logical(jnp.minimum(qi * t + (t - 1) + reach, n_rows - 1), log2_t) + 1

    @pl.when(qi == 1)
    def _():
        r = lax.broadcasted_iota(jnp.int32, (t, t), 0)
        c = lax.broadcasted_iota(jnp.int32, (t, t), 1)
        left = -sigma * (r - c).astype(F32)
        bias_ref[BIAS_LEFT] = left
        bias_ref[BIAS_DIAG] = -jnp.abs(left)
        bias_ref[BIAS_RIGHT] = -left
        bias_ref[BIAS_FRONT] = jnp.where(c >= FRONT_PAD, left, MASKED_LOGIT)

    def lane_sums(x):
        ones = jnp.ones((x.shape[1], LANES), BF16)
        hi_part = x.astype(BF16)
        lo_part = (x - hi_part.astype(F32)).astype(BF16)
        return (jnp.dot(hi_part, ones, preferred_element_type=F32)
                + jnp.dot(lo_part, ones, preferred_element_type=F32))

    l_ref[...] = jnp.zeros(l_ref.shape, F32)
    acc_ref[...] = jnp.zeros(acc_ref.shape, F32)

    def tile_class(j):
        side = jnp.where(j < qi, BIAS_LEFT, jnp.where(j == qi, BIAS_DIAG, BIAS_RIGHT))
        return jnp.where(j == 0, BIAS_FRONT, side)

    def tile_offset(j):
        gap = jnp.full((1, LANES), jnp.abs(qi - j) * t, jnp.int32).astype(F32)
        return gap * (-sigma)

    def bounded_block(j0, n_blk):
        start = pl.multiple_of(j0 * t, t)
        width = n_blk * t
        classes = [tile_class(j0 + u) for u in range(n_blk)]
        offsets = [tile_offset(j0 + u) for u in range(n_blk)]
        vv = v_ref[0, 0, pl.ds(start, width), :]
        for m in range(2):
            kk = k_ref[0, 0, m, pl.ds(start, width), :]
            s = lax.dot_general(q_ref[0, 0, m], kk, _NT_DIMS, preferred_element_type=F32)
            row_sum = None
            ps = []
            for u in range(n_blk):
                shift = offsets[u] - bounds[m]
                for cc in range(n_chunks):
                    sl = slice(cc * LANES, (cc + 1) * LANES)
                    su = s[:, u * t + cc * LANES:u * t + (cc + 1) * LANES]
                    p = jnp.exp2(su + bias_ref[classes[u], :, sl] + shift)
                    row_sum = p if row_sum is None else row_sum + p
                    ps.append(p.astype(BF16))
            l_ref[m] += row_sum
            acc_ref[m] += jnp.dot(jnp.concatenate(ps, axis=1), vv, preferred_element_type=F32)

    n_steps = hi - lo
    log2_wide = WIDEST_KEY_BLOCK.bit_length() - 1
    n_wide = lax.shift_right_logical(n_steps, log2_wide)

    def wide_body(i, carry):
        bounded_block(lo + WIDEST_KEY_BLOCK * i, WIDEST_KEY_BLOCK)
        return carry

    lax.fori_loop(0, n_wide, wide_body, 0)
    done = n_wide * WIDEST_KEY_BLOCK
    for width in (WIDEST_KEY_BLOCK >> s for s in range(1, log2_wide + 1)):
        covered = done + jnp.bitwise_and(n_steps, (WIDEST_KEY_BLOCK - 1) & ~(2 * width - 1))

        @pl.when(jnp.bitwise_and(n_steps, width) != 0)
        def _(covered=covered, width=width):
            bounded_block(lo + covered, width)

    def finalize():
        lam = (jnp.exp(jnp.sum(lq1_ref[...] * lk1_ref[...], axis=-1, keepdims=True))
               - jnp.exp(jnp.sum(lq2_ref[...] * lk2_ref[...], axis=-1, keepdims=True))
               + lambda_init)
        inv = [1.0 / ls_ref[m] for m in range(2)]
        o = (acc_ref[0] * jnp.concatenate([inv[0], inv[0]], axis=1)
             - lam * (acc_ref[1] * jnp.concatenate([inv[1], inv[1]], axis=1)))
        r = lax.rsqrt(lane_sums(o * o) * (1.0 / V_HEAD_DIM) + RMS_EPS)
        o = o * jnp.concatenate([r, r], axis=1) * (subg_ref[...] * (1.0 - lambda_init))
        o_ref[0] = o.astype(BF16)

    for m in range(2):
        ls_ref[m] = lane_sums(l_ref[m])
    smallest = jnp.min(jnp.minimum(ls_ref[0], ls_ref[1]))
    finalize()

    @pl.when(smallest < MIN_BOUNDED_ROW_SUM)
    def _():
        m_ref[...] = jnp.full(m_ref.shape, MASKED_LOGIT, F32)
        lrun_ref[...] = jnp.zeros(lrun_ref.shape, F32)
        acc_ref[...] = jnp.zeros(acc_ref.shape, F32)

        def online_step(j, carry):
            start = pl.multiple_of(j * t, t)
            cls = tile_class(j)
            off = tile_offset(j)[:, :1]
            vv = v_ref[0, 0, pl.ds(start, t), :]
            for m in range(2):
                kk = k_ref[0, 0, m, pl.ds(start, t), :]
                s = lax.dot_general(q_ref[0, 0, m], kk, _NT_DIMS, preferred_element_type=F32)
                s = s + bias_ref[cls] + off
                m_prev = m_ref[m]
                m_new = jnp.maximum(m_prev, jnp.max(s, axis=-1, keepdims=True))
                alpha = jnp.exp2(m_prev - m_new)
                p = jnp.exp2(s - m_new)
                lrun_ref[m] = alpha * lrun_ref[m] + jnp.sum(p, axis=-1, keepdims=True)
                acc_ref[m] = alpha * acc_ref[m] + jnp.dot(p.astype(BF16), vv,
                                                          preferred_element_type=F32)
                m_ref[m] = m_new
            return carry

        lax.fori_loop(lo, hi, online_step, 0)
        for m in range(2):
            ls_ref[m] = jnp.broadcast_to(lrun_ref[m], (t, LANES))
        finalize()


def _attention_bounds(stats, slopes, bsz, lp):
    stats = stats[:, :, 0].reshape(bsz, -1, N_STAT_ROWS)
    norms = jnp.sqrt(jnp.max(stats[:, :, :4 * N_HEADS], axis=1)).reshape(bsz, N_HEADS, 2, 2)
    lowest_self = jnp.min(stats[:, :, 4 * N_HEADS:], axis=1).reshape(bsz, N_HEADS, 2)
    product = norms[:, :, 0] * norms[:, :, 1] * BOUND_SLACK
    spread = jnp.max(product * BOUND_SLACK - lowest_self, axis=-1) + 1.0
    reach = (SKIP_LOG2_THRESHOLD + spread) / (slopes * LOG2_E)[None, :]
    reach = jnp.ceil(jnp.minimum(reach, float(lp))).astype(jnp.int32) + 1
    return product.reshape(-1), reach.reshape(-1)


def _attention_layer(q, k, v, slopes, bound, reach, lq1, lk1, lq2, lk2, subg, lambda_init):
    bsz, n_heads, _, lp, dh = q.shape
    tq = SEQ_TILE
    n_q = lp // tq - 1
    dv = v.shape[-1]
    kv_bytes = 2 * (2 * lp * dh * 2 + lp * dv * 2)
    scratch = (N_BIAS_CLASSES * tq * tq + 4 * tq * LANES + 2 * tq * dv) * 4
    blocks = kv_bytes + 2 * 2 * tq * dh * 2 + 2 * tq * dv * 2 + scratch
    vec = lambda b, h, i, *_: (0, 0)
    grid_spec = pltpu.PrefetchScalarGridSpec(
        num_scalar_prefetch=3,
        grid=(bsz, n_heads, n_q),
        in_specs=[
            pl.BlockSpec((1, 1, 2, tq, dh), lambda b, h, i, *_: (b, h, 0, i + 1, 0)),
            pl.BlockSpec((1, 1, 2, lp, dh), lambda b, h, i, *_: (b, h, 0, 0, 0)),
            pl.BlockSpec((1, 1, lp, dv), lambda b, h, i, *_: (b, h, 0, 0)),
            pl.BlockSpec((1, dh), vec), pl.BlockSpec((1, dh), vec),
            pl.BlockSpec((1, dh), vec), pl.BlockSpec((1, dh), vec),
            pl.BlockSpec((1, dv), vec),
        ],
        out_specs=pl.BlockSpec((1, tq, dv), lambda b, h, i, *_: (b, i, h)),
        scratch_shapes=[
            pltpu.VMEM((N_BIAS_CLASSES, tq, tq), F32),
            pltpu.VMEM((2, tq, LANES), F32),
            pltpu.VMEM((2, tq, dv), F32),
            pltpu.VMEM((2, tq, LANES), F32),
            pltpu.VMEM((2, tq, 1), F32),
            pltpu.VMEM((2, tq, 1), F32),
        ],
    )
    return pl.pallas_call(
        functools.partial(_attn_kernel, lambda_init=lambda_init),
        out_shape=jax.ShapeDtypeStruct((bsz, n_q * tq, n_heads * dv), BF16),
        grid_spec=grid_spec,
        compiler_params=pltpu.CompilerParams(
            dimension_semantics=("arbitrary", "arbitrary", "arbitrary"),
            vmem_limit_bytes=_vmem_limit(blocks + 6 * 4 * tq * tq * 4)),
        name="diff_attention",
    )(slopes, bound, reach, q, k, v, lq1.reshape(1, dh), lk1.reshape(1, dh),
      lq2.reshape(1, dh), lk2.reshape(1, dh), subg.reshape(1, dv))


def _oproj_kernel(h_ref, a_ref, w_ref, out_ref):
    out_ref[...] = h_ref[...] + jnp.dot(a_ref[...], w_ref[...], preferred_element_type=F32)


def _oproj_layer(h, attn, w_bf16, h_tile_of):
    n, d = attn.shape
    tm = TOKEN_TILE
    blocks = 4 * tm * d * 4 + 2 * tm * d * 2 + 2 * d * d * 2 + tm * d * 4
    return pl.pallas_call(
        _oproj_kernel,
        out_shape=jax.ShapeDtypeStruct((n, d), F32),
        grid=(n // tm,),
        in_specs=[
            pl.BlockSpec((tm, d), lambda i: (h_tile_of(i), 0)),
            pl.BlockSpec((tm, d), lambda i: (i, 0)),
            pl.BlockSpec((d, d), lambda i: (0, 0)),
        ],
        out_specs=pl.BlockSpec((tm, d), lambda i: (i, 0)),
        compiler_params=pltpu.CompilerParams(
            dimension_semantics=("parallel",),
            vmem_limit_bytes=_vmem_limit(blocks)),
        name="attn_out_proj",
    )(h, attn, w_bf16)


def _encode(x, p):
    bsz, s, d = x.shape
    assert s % SEQ_TILE == 0 and SEQ_TILE == TOKEN_TILE
    lp = s + SEQ_TILE
    assert (bsz * lp) % WIDE_TOKEN_TILE == 0 and (bsz * s) % WIDE_TOKEN_TILE == 0
    tiles_in, tiles_out = lp // TOKEN_TILE, s // TOKEN_TILE
    front = jnp.concatenate([jnp.zeros((FRONT_PAD, d), x.dtype), p["meta_tokens"]], axis=0)

    h = _pool_layer(x, front, p["mixer_norm_g"][0], p["pool_w"][0], p["pool_scale"][0])
    h = h.reshape(bsz * lp, d)
    h = _ffn_layer(h, p["ffn_norm_g"][0], p["w_gate"], p["w_up"], p["w_down"],
                   p["final_norm_g"], layer=0, final_norm=False)

    lambda_init = 0.8 - 0.6 * math.exp(-0.3 * 1)
    slopes = 2.0 ** (-8.0 * (jnp.arange(N_HEADS, dtype=F32) + 1.0) / N_HEADS)
    q, k, v, sq = _qkv_layer(h, p["mixer_norm_g"][1], p["w_qkv"][0], bsz, lp)
    bound, reach = _attention_bounds(sq, slopes, bsz, lp)
    o = _attention_layer(q, k, v, slopes, bound, reach, p["lambda_q1"][0], p["lambda_k1"][0],
                         p["lambda_q2"][0], p["lambda_k2"][0], p["subln_g"][0], lambda_init)
    h = _oproj_layer(h, o.reshape(bsz * s, d), p["w_o"][0],
                     h_tile_of=lambda i: (i // tiles_out) * tiles_in + i % tiles_out + 1)
    y = _ffn_layer(h, p["ffn_norm_g"][1], p["w_gate"], p["w_up"], p["w_down"],
                   p["final_norm_g"], layer=1, final_norm=True)
    return y.reshape(bsz, s, d)


def kernel(x_prompt, x_sample, meta_tokens, mixer_norm_g, pool_w, pool_scale, w_qkv, lambda_q1,
           lambda_k1, lambda_q2, lambda_k2, subln_g, w_o, ffn_norm_g, w_gate, w_up, w_down,
           final_norm_g):
    p = dict(
        meta_tokens=meta_tokens, mixer_norm_g=mixer_norm_g, pool_scale=pool_scale,
        lambda_q1=lambda_q1, lambda_k1=lambda_k1, lambda_q2=lambda_q2, lambda_k2=lambda_k2,
        subln_g=subln_g, ffn_norm_g=ffn_norm_g, final_norm_g=final_norm_g,
        pool_w=pool_w.astype(BF16), w_qkv=w_qkv.astype(BF16), w_o=w_o.astype(BF16),
        w_gate=w_gate.astype(BF16), w_up=w_up.astype(BF16), w_down=w_down.astype(BF16),
    )
    return (_encode(x_prompt, p), _encode(x_sample, p))
```

```python
import functools
import math

import jax
import jax.numpy as jnp
from jax import lax
from jax.experimental import pallas as pl
from jax.experimental.pallas import tpu as pltpu

D_MODEL = 2048
N_META = 16
N_POOL_GROUPS = 4
POOL_WINDOWS = (2, 4, 8, 16)
POOL_GROUP_DIM = D_MODEL // N_POOL_GROUPS
N_HEADS = 8
HEAD_DIM = 128
V_HEAD_DIM = 2 * HEAD_DIM
D_FF = 5632
RMS_EPS = 1e-6

F32 = jnp.float32
BF16 = jnp.bfloat16

LANES = 128
V7X_VMEM_BYTES = 64 * 1024 * 1024
V7X_VMEM_COMPILER_RESERVE = 8 * 1024 * 1024

SEQ_TILE = 512
FRONT_PAD = SEQ_TILE - N_META
TOKEN_TILE = 512
WIDE_TOKEN_TILE = 1024
FF_TILE = 512
POOL_HALO = 8
WIDEST_KEY_BLOCK = 16

LOG2_E = math.log2(math.e)
QK_LOG2_SCALE = HEAD_DIM ** -0.5 * LOG2_E
SKIP_LOG2_THRESHOLD = 90.0 * LOG2_E
MASKED_LOGIT = -1e30
MIN_BOUNDED_ROW_SUM = 2.0 ** -60
BOUND_SLACK = 1.0 + 2.0 ** -6


def _vmem_limit(block_bytes):
    return min(V7X_VMEM_BYTES - V7X_VMEM_COMPILER_RESERVE,
               int(block_bytes) + V7X_VMEM_COMPILER_RESERVE)


def _rmsnorm(x, g):
    ms = jnp.mean(x * x, axis=-1, keepdims=True)
    return x * lax.rsqrt(ms + RMS_EPS) * g


def _pool_kernel(cur_ref, prev_ref, next_ref, front_ref, g_ref, w_ref, scale_ref, out_ref, *,
                 n_pos):
    i = pl.program_id(1)
    n_tiles = pl.num_programs(1)
    tile = out_ref.shape[1]
    x = jnp.where(i == 0, front_ref[...], cur_ref[0])
    front_tail = front_ref[tile - POOL_HALO:, :]
    halo_lo = jnp.where(i >= 2, prev_ref[0], jnp.where(i == 1, front_tail, 0.0))
    halo_hi = jnp.where(i < n_tiles - 1, next_ref[0], 0.0)
    ext = jnp.concatenate([halo_lo, x, halo_hi], axis=0)
    hn = _rmsnorm(ext, g_ref[...])
    n_ext = ext.shape[0]

    def shifted(a, d):
        return pltpu.roll(a, (-d) % n_ext, axis=0)

    pos = i * tile - FRONT_PAD + lax.broadcasted_iota(jnp.int32, (tile, 1), 0)
    valid = pos >= 0
    for gi, w in enumerate(POOL_WINDOWS):
        cols = slice(gi * POOL_GROUP_DIM, (gi + 1) * POOL_GROUP_DIM)
        e = hn[:, cols]
        s = shifted(e, -1) + e
        reach = 1
        while 2 * reach < w:
            s = shifted(s, -reach) + shifted(s, reach)
            reach *= 2
        win = s[POOL_HALO:POOL_HALO + tile]
        cnt = jnp.minimum(pos + w // 2, n_pos) - jnp.maximum(pos - w // 2, 0)
        cnt = jnp.maximum(cnt, 1).astype(F32)
        pooled = win / cnt - e[POOL_HALO:POOL_HALO + tile]
        y = jnp.dot(pooled.astype(BF16), w_ref[gi], preferred_element_type=F32)
        y = y * scale_ref[:, cols]
        out_ref[0, :, cols] = jnp.where(valid, x[:, cols] + y, 0.0)


def _pool_layer(x, front, g, w_bf16, scale):
    bsz, s, d = x.shape
    tile = SEQ_TILE
    nt = s // tile + 1
    hb = tile // POOL_HALO
    n_halo_blocks = s // POOL_HALO
    blk = tile * d * 4
    return pl.pallas_call(
        functools.partial(_pool_kernel, n_pos=s + N_META),
        out_shape=jax.ShapeDtypeStruct((bsz, nt * tile, d), F32),
        grid=(bsz, nt),
        in_specs=[
            pl.BlockSpec((1, tile, d), lambda b, i: (b, jnp.maximum(i - 1, 0), 0)),
            pl.BlockSpec((1, POOL_HALO, d),
                         lambda b, i: (b, jnp.maximum((i - 1) * hb - 1, 0), 0)),
            pl.BlockSpec((1, POOL_HALO, d),
                         lambda b, i: (b, jnp.minimum(i * hb, n_halo_blocks - 1), 0)),
            pl.BlockSpec((tile, d), lambda b, i: (0, 0)),
            pl.BlockSpec((1, d), lambda b, i: (0, 0)),
            pl.BlockSpec((N_POOL_GROUPS, POOL_GROUP_DIM, POOL_GROUP_DIM), lambda b, i: (0, 0, 0)),
            pl.BlockSpec((1, d), lambda b, i: (0, 0)),
        ],
        out_specs=pl.BlockSpec((1, tile, d), lambda b, i: (b, i, 0)),
        compiler_params=pltpu.CompilerParams(
            dimension_semantics=("parallel", "arbitrary"),
            vmem_limit_bytes=_vmem_limit(6 * blk + 4 * w_bf16.size + 4 * blk)),
        name="pool_mixer",
    )(x, x, x, front, g.reshape(1, d), w_bf16, scale.reshape(1, d))


def _ffn_kernel(x_ref, g_ref, wg_ref, wu_ref, wd_ref, gf_ref, o_ref, hn_ref, *, final_norm):
    c = pl.program_id(1)

    def ffn_chunk(hn):
        gate = jnp.dot(hn, wg_ref[...], preferred_element_type=F32)
        up = jnp.dot(hn, wu_ref[...], preferred_element_type=F32)
        act = (gate * jax.nn.sigmoid(gate) * up).astype(BF16)
        return jnp.dot(act, wd_ref[...], preferred_element_type=F32)

    @pl.when(c == 0)
    def _():
        x = x_ref[...]
        hn = _rmsnorm(x, g_ref[...]).astype(BF16)
        hn_ref[...] = hn
        o_ref[...] = x + ffn_chunk(hn)

    @pl.when(c > 0)
    def _():
        o_ref[...] += ffn_chunk(hn_ref[...])

    if final_norm:
        @pl.when(c == pl.num_programs(1) - 1)
        def _():
            o_ref[...] = _rmsnorm(o_ref[...], gf_ref[...])


def _ffn_layer(x, g, wg, wu, wd, gf, *, layer, final_norm):
    n, d = x.shape
    tm, tf = WIDE_TOKEN_TILE, FF_TILE
    dff = wg.shape[2]
    blocks = 4 * tm * d * 4 + tm * d * 2 + 2 * 3 * d * tf * 2 + 3 * tm * tf * 4
    return pl.pallas_call(
        functools.partial(_ffn_kernel, final_norm=final_norm),
        out_shape=jax.ShapeDtypeStruct((n, d), F32),
        grid=(n // tm, dff // tf),
        in_specs=[
            pl.BlockSpec((tm, d), lambda i, c: (i, 0)),
            pl.BlockSpec((1, d), lambda i, c: (0, 0)),
            pl.BlockSpec((None, d, tf), lambda i, c: (layer, 0, c)),
            pl.BlockSpec((None, d, tf), lambda i, c: (layer, 0, c)),
            pl.BlockSpec((None, tf, d), lambda i, c: (layer, c, 0)),
            pl.BlockSpec((1, d), lambda i, c: (0, 0)),
        ],
        out_specs=pl.BlockSpec((tm, d), lambda i, c: (i, 0)),
        scratch_shapes=[pltpu.VMEM((tm, d), BF16)],
        compiler_params=pltpu.CompilerParams(
            dimension_semantics=("parallel", "arbitrary"),
            vmem_limit_bytes=_vmem_limit(blocks)),
        name="swiglu_ffn_final" if final_norm else "swiglu_ffn",
    )(x, g.reshape(1, d), wg, wu, wd, gf.reshape(1, d))


N_STAT_ROWS = 6 * N_HEADS


def _qkv_kernel(x_ref, g_ref, w_ref, q_ref, k_ref, v_ref, st_ref):
    hn = _rmsnorm(x_ref[...], g_ref[...]).astype(BF16)

    def put(row, value):
        st_ref[0, row:row + 1, :] = jnp.broadcast_to(value, (1, LANES))

    for h in range(N_HEADS):
        qk = []
        for part, ref in ((0, q_ref), (1, k_ref)):
            c0 = part * D_MODEL + h * V_HEAD_DIM
            r = jnp.dot(hn, w_ref[:, c0:c0 + V_HEAD_DIM], preferred_element_type=F32)
            if part == 0:
                r = r * QK_LOG2_SCALE
            qk.append(r)
            for m in range(2):
                rm = r[:, m * HEAD_DIM:(m + 1) * HEAD_DIM]
                ref[0, h, m] = rm.astype(BF16)
                put((2 * h + part) * 2 + m,
                    jnp.max(jnp.sum(rm * rm, axis=-1, keepdims=True), axis=0, keepdims=True))
        for m in range(2):
            sl = slice(m * HEAD_DIM, (m + 1) * HEAD_DIM)
            self_logit = jnp.sum(qk[0][:, sl] * qk[1][:, sl], axis=-1, keepdims=True)
            put(4 * N_HEADS + 2 * h + m, jnp.min(self_logit, axis=0, keepdims=True))
        c0 = 2 * D_MODEL + h * V_HEAD_DIM
        v_ref[0, h] = jnp.dot(hn, w_ref[:, c0:c0 + V_HEAD_DIM],
                              preferred_element_type=F32).astype(BF16)


def _qkv_layer(x, g, w_bf16, bsz, lp):
    n, d = x.shape
    tm = TOKEN_TILE
    nt = lp // tm
    qk_shape = jax.ShapeDtypeStruct((bsz, N_HEADS, 2, lp, HEAD_DIM), BF16)
    v_shape = jax.ShapeDtypeStruct((bsz, N_HEADS, lp, V_HEAD_DIM), BF16)
    sq_shape = jax.ShapeDtypeStruct((n // tm, N_STAT_ROWS, LANES), F32)
    blocks = 2 * tm * d * 4 + w_bf16.size * 2 + 2 * 3 * tm * d * 2 + tm * d * 2
    return pl.pallas_call(
        _qkv_kernel,
        out_shape=(qk_shape, qk_shape, v_shape, sq_shape),
        grid=(n // tm,),
        in_specs=[
            pl.BlockSpec((tm, d), lambda i: (i, 0)),
            pl.BlockSpec((1, d), lambda i: (0, 0)),
            pl.BlockSpec((d, 3 * d), lambda i: (0, 0), pipeline_mode=pl.Buffered(1)),
        ],
        out_specs=(
            pl.BlockSpec((1, N_HEADS, 2, tm, HEAD_DIM), lambda i: (i // nt, 0, 0, i % nt, 0)),
            pl.BlockSpec((1, N_HEADS, 2, tm, HEAD_DIM), lambda i: (i // nt, 0, 0, i % nt, 0)),
            pl.BlockSpec((1, N_HEADS, tm, V_HEAD_DIM), lambda i: (i // nt, 0, i % nt, 0)),
            pl.BlockSpec((1, N_STAT_ROWS, LANES), lambda i: (i, 0, 0)),
        ),
        compiler_params=pltpu.CompilerParams(
            dimension_semantics=("parallel",),
            vmem_limit_bytes=_vmem_limit(blocks)),
        name="qkv_proj",
    )(x, g.reshape(1, d), w_bf16)


BIAS_LEFT, BIAS_DIAG, BIAS_RIGHT, BIAS_FRONT = 0, 1, 2, 3
N_BIAS_CLASSES = 4
_NT_DIMS = (((1,), (1,)), ((), ()))


def _attn_kernel(slope_ref, bound_ref, reach_ref, q_ref, k_ref, v_ref, lq1_ref, lk1_ref, lq2_ref,
                 lk2_ref, subg_ref, o_ref, bias_ref, l_ref, acc_ref, ls_ref, m_ref, lrun_ref, *,
                 lambda_init):
    b, h = pl.program_id(0), pl.program_id(1)
    qi = pl.program_id(2) + 1
    t = q_ref.shape[3]
    n_rows = k_ref.shape[3]
    n_chunks = t // LANES
    log2_t = t.bit_length() - 1
    assert t == 1 << log2_t
    sigma = slope_ref[h] * LOG2_E
    bounds = [bound_ref[(b * N_HEADS + h) * 2 + m] for m in range(2)]
    reach = reach_ref[b * N_HEADS + h]
    lo = lax.shift_right_logical(jnp.maximum(qi * t - reach, 0), log2_t)
    hi = lax.shift_right_logical(jnp.minimum(qi * t + (t - 1) + reach, n_rows - 1), log2_t) + 1

    @pl.when(qi == 1)
    def _():
        r = lax.broadcasted_iota(jnp.int32, (t, t), 0)
        c = lax.broadcasted_iota(jnp.int32, (t, t), 1)
        left = -sigma * (r - c).astype(F32)
        bias_ref[BIAS_LEFT] = left
        bias_ref[BIAS_DIAG] = -jnp.abs(left)
        bias_ref[BIAS_RIGHT] = -left
        bias_ref[BIAS_FRONT] = jnp.where(c >= FRONT_PAD, left, MASKED_LOGIT)

    def lane_sums(x):
        ones = jnp.ones((x.shape[1], LANES), BF16)
        hi_part = x.astype(BF16)
        lo_part = (x - hi_part.astype(F32)).astype(BF16)
        return (jnp.dot(hi_part, ones, preferred_element_type=F32)
                + jnp.dot(lo_part, ones, preferred_element_type=F32))

    l_ref[...] = jnp.zeros(l_ref.shape, F32)
    acc_ref[...] = jnp.zeros(acc_ref.shape, F32)

    def tile_class(j):
        side = jnp.where(j < qi, BIAS_LEFT, jnp.where(j == qi, BIAS_DIAG, BIAS_RIGHT))
        return jnp.where(j == 0, BIAS_FRONT, side)

    def tile_offset(j):
        gap = jnp.full((1, LANES), jnp.abs(qi - j) * t, jnp.int32).astype(F32)
        return gap * (-sigma)

    def bounded_block(j0, n_blk):
        start = pl.multiple_of(j0 * t, t)
        width = n_blk * t
        classes = [tile_class(j0 + u) for u in range(n_blk)]
        offsets = [tile_offset(j0 + u) for u in range(n_blk)]
        vv = v_ref[0, 0, pl.ds(start, width), :]
        for m in range(2):
            kk = k_ref[0, 0, m, pl.ds(start, width), :]
            s = lax.dot_general(q_ref[0, 0, m], kk, _NT_DIMS, preferred_element_type=F32)
            row_sum = None
            ps = []
            for u in range(n_blk):
                shift = offsets[u] - bounds[m]
                for cc in range(n_chunks):
                    sl = slice(cc * LANES, (cc + 1) * LANES)
                    su = s[:, u * t + cc * LANES:u * t + (cc + 1) * LANES]
                    p = jnp.exp2(su + bias_ref[classes[u], :, sl] + shift)
                    row_sum = p if row_sum is None else row_sum + p
                    ps.append(p.astype(BF16))
            l_ref[m] += row_sum
            acc_ref[m] += jnp.dot(jnp.concatenate(ps, axis=1), vv, preferred_element_type=F32)

    n_steps = hi - lo
    log2_wide = WIDEST_KEY_BLOCK.bit_length() - 1
    n_wide = lax.shift_right_logical(n_steps, log2_wide)

    def wide_body(i, carry):
        bounded_block(lo + WIDEST_KEY_BLOCK * i, WIDEST_KEY_BLOCK)
        return carry

    lax.fori_loop(0, n_wide, wide_body, 0)
    done = n_wide * WIDEST_KEY_BLOCK
    for width in (WIDEST_KEY_BLOCK >> s for s in range(1, log2_wide + 1)):
        covered = done + jnp.bitwise_and(n_steps, (WIDEST_KEY_BLOCK - 1) & ~(2 * width - 1))

        @pl.when(jnp.bitwise_and(n_steps, width) != 0)
        def _(covered=covered, width=width):
            bounded_block(lo + covered, width)

    def finalize():
        lam = (jnp.exp(jnp.sum(lq1_ref[...] * lk1_ref[...], axis=-1, keepdims=True))
               - jnp.exp(jnp.sum(lq2_ref[...] * lk2_ref[...], axis=-1, keepdims=True))
               + lambda_init)
        inv = [1.0 / ls_ref[m] for m in range(2)]
        o = (acc_ref[0] * jnp.concatenate([inv[0], inv[0]], axis=1)
             - lam * (acc_ref[1] * jnp.concatenate([inv[1], inv[1]], axis=1)))
        r = lax.rsqrt(lane_sums(o * o) * (1.0 / V_HEAD_DIM) + RMS_EPS)
        o = o * jnp.concatenate([r, r], axis=1) * (subg_ref[...] * (1.0 - lambda_init))
        o_ref[0] = o.astype(BF16)

    for m in range(2):
        ls_ref[m] = lane_sums(l_ref[m])
    smallest = jnp.min(jnp.minimum(ls_ref[0], ls_ref[1]))
    finalize()

    @pl.when(smallest < MIN_BOUNDED_ROW_SUM)
    def _():
        m_ref[...] = jnp.full(m_ref.shape, MASKED_LOGIT, F32)
        lrun_ref[...] = jnp.zeros(lrun_ref.shape, F32)
        acc_ref[...] = jnp.zeros(acc_ref.shape, F32)

        def online_step(j, carry):
            start = pl.multiple_of(j * t, t)
            cls = tile_class(j)
            off = tile_offset(j)[:, :1]
            vv = v_ref[0, 0, pl.ds(start, t), :]
            for m in range(2):
                kk = k_ref[0, 0, m, pl.ds(start, t), :]
                s = lax.dot_general(q_ref[0, 0, m], kk, _NT_DIMS, preferred_element_type=F32)
                s = s + bias_ref[cls] + off
                m_prev = m_ref[m]
                m_new = jnp.maximum(m_prev, jnp.max(s, axis=-1, keepdims=True))
                alpha = jnp.exp2(m_prev - m_new)
                p = jnp.exp2(s - m_new)
                lrun_ref[m] = alpha * lrun_ref[m] + jnp.sum(p, axis=-1, keepdims=True)
                acc_ref[m] = alpha * acc_ref[m] + jnp.dot(p.astype(BF16), vv,
                                                          preferred_element_type=F32)
                m_ref[m] = m_new
            return carry

        lax.fori_loop(lo, hi, online_step, 0)
        for m in range(2):
            ls_ref[m] = jnp.broadcast_to(lrun_ref[m], (t, LANES))
        finalize()


def _attention_bounds(stats, slopes, bsz, lp):
    stats = stats[:, :, 0].reshape(bsz, -1, N_STAT_ROWS)
    norms = jnp.sqrt(jnp.max(stats[:, :, :4 * N_HEADS], axis=1)).reshape(bsz, N_HEADS, 2, 2)
    lowest_self = jnp.min(stats[:, :, 4 * N_HEADS:], axis=1).reshape(bsz, N_HEADS, 2)
    product = norms[:, :, 0] * norms[:, :, 1] * BOUND_SLACK
    spread = jnp.max(product * BOUND_SLACK - lowest_self, axis=-1) + 1.0
    reach = (SKIP_LOG2_THRESHOLD + spread) / (slopes * LOG2_E)[None, :]
    reach = jnp.ceil(jnp.minimum(reach, float(lp))).astype(jnp.int32) + 1
    return product.reshape(-1), reach.reshape(-1)


def _attention_layer(q, k, v, slopes, bound, reach, lq1, lk1, lq2, lk2, subg, lambda_init):
    bsz, n_heads, _, lp, dh = q.shape
    tq = SEQ_TILE
    n_q = lp // tq - 1
    dv = v.shape[-1]
    kv_bytes = 2 * (2 * lp * dh * 2 + lp * dv * 2)
    scratch = (N_BIAS_CLASSES * tq * tq + 4 * tq * LANES + 2 * tq * dv) * 4
    blocks = kv_bytes + 2 * 2 * tq * dh * 2 + 2 * tq * dv * 2 + scratch
    vec = lambda b, h, i, *_: (0, 0)
    grid_spec = pltpu.PrefetchScalarGridSpec(
        num_scalar_prefetch=3,
        grid=(bsz, n_heads, n_q),
        in_specs=[
            pl.BlockSpec((1, 1, 2, tq, dh), lambda b, h, i, *_: (b, h, 0, i + 1, 0)),
            pl.BlockSpec((1, 1, 2, lp, dh), lambda b, h, i, *_: (b, h, 0, 0, 0)),
            pl.BlockSpec((1, 1, lp, dv), lambda b, h, i, *_: (b, h, 0, 0)),
            pl.BlockSpec((1, dh), vec), pl.BlockSpec((1, dh), vec),
            pl.BlockSpec((1, dh), vec), pl.BlockSpec((1, dh), vec),
            pl.BlockSpec((1, dv), vec),
        ],
        out_specs=pl.BlockSpec((1, tq, dv), lambda b, h, i, *_: (b, i, h)),
        scratch_shapes=[
            pltpu.VMEM((N_BIAS_CLASSES, tq, tq), F32),
            pltpu.VMEM((2, tq, LANES), F32),
            pltpu.VMEM((2, tq, dv), F32),
            pltpu.VMEM((2, tq, LANES), F32),
            pltpu.VMEM((2, tq, 1), F32),
            pltpu.VMEM((2, tq, 1), F32),
        ],
    )
    return pl.pallas_call(
        functools.partial(_attn_kernel, lambda_init=lambda_init),
        out_shape=jax.ShapeDtypeStruct((bsz, n_q * tq, n_heads * dv), BF16),
        grid_spec=grid_spec,
        compiler_params=pltpu.CompilerParams(
            dimension_semantics=("arbitrary", "arbitrary", "arbitrary"),
            vmem_limit_bytes=_vmem_limit(blocks + 6 * 4 * tq * tq * 4)),
        name="diff_attention",
    )(slopes, bound, reach, q, k, v, lq1.reshape(1, dh), lk1.reshape(1, dh),
      lq2.reshape(1, dh), lk2.reshape(1, dh), subg.reshape(1, dv))


def _oproj_kernel(h_ref, a_ref, w_ref, out_ref):
    out_ref[...] = h_ref[...] + jnp.dot(a_ref[...], w_ref[...], preferred_element_type=F32)


def _oproj_layer(h, attn, w_bf16, h_tile_of):
    n, d = attn.shape
    tm = TOKEN_TILE
    blocks = 4 * tm * d * 4 + 2 * tm * d * 2 + 2 * d * d * 2 + tm * d * 4
    return pl.pallas_call(
        _oproj_kernel,
        out_shape=jax.ShapeDtypeStruct((n, d), F32),
        grid=(n // tm,),
        in_specs=[
            pl.BlockSpec((tm, d), lambda i: (h_tile_of(i), 0)),
            pl.BlockSpec((tm, d), lambda i: (i, 0)),
            pl.BlockSpec((d, d), lambda i: (0, 0)),
        ],
        out_specs=pl.BlockSpec((tm, d), lambda i: (i, 0)),
        compiler_params=pltpu.CompilerParams(
            dimension_semantics=("parallel",),
            vmem_limit_bytes=_vmem_limit(blocks)),
        name="attn_out_proj",
    )(h, attn, w_bf16)


def _encode(x, p):
    bsz, s, d = x.shape
    assert s % SEQ_TILE == 0 and SEQ_TILE == TOKEN_TILE
    lp = s + SEQ_TILE
    assert (bsz * lp) % WIDE_TOKEN_TILE == 0 and (bsz * s) % WIDE_TOKEN_TILE == 0
    tiles_in, tiles_out = lp // TOKEN_TILE, s // TOKEN_TILE
    front = jnp.concatenate([jnp.zeros((FRONT_PAD, d), x.dtype), p["meta_tokens"]], axis=0)

    h = _pool_layer(x, front, p["mixer_norm_g"][0], p["pool_w"][0], p["pool_scale"][0])
    h = h.reshape(bsz * lp, d)
    h = _ffn_layer(h, p["ffn_norm_g"][0], p["w_gate"], p["w_up"], p["w_down"],
                   p["final_norm_g"], layer=0, final_norm=False)

    lambda_init = 0.8 - 0.6 * math.exp(-0.3 * 1)
    slopes = 2.0 ** (-8.0 * (jnp.arange(N_HEADS, dtype=F32) + 1.0) / N_HEADS)
    q, k, v, sq = _qkv_layer(h, p["mixer_norm_g"][1], p["w_qkv"][0], bsz, lp)
    bound, reach = _attention_bounds(sq, slopes, bsz, lp)
    o = _attention_layer(q, k, v, slopes, bound, reach, p["lambda_q1"][0], p["lambda_k1"][0],
                         p["lambda_q2"][0], p["lambda_k2"][0], p["subln_g"][0], lambda_init)
    h = _oproj_layer(h, o.reshape(bsz * s, d), p["w_o"][0],
                     h_tile_of=lambda i: (i // tiles_out) * tiles_in + i % tiles_out + 1)
    y = _ffn_layer(h, p["ffn_norm_g"][1], p["w_gate"], p["w_up"], p["w_down"],
                   p["final_norm_g"], layer=1, final_norm=True)
    return y.reshape(bsz, s, d)


def kernel(x_prompt, x_sample, meta_tokens, mixer_norm_g, pool_w, pool_scale, w_qkv, lambda_q1,
           lambda_k1, lambda_q2, lambda_k2, subln_g, w_o, ffn_norm_g, w_gate, w_up, w_down,
           final_norm_g):
    p = dict(
        meta_tokens=meta_tokens, mixer_norm_g=mixer_norm_g, pool_scale=pool_scale,
        lambda_q1=lambda_q1, lambda_k1=lambda_k1, lambda_q2=lambda_q2, lambda_k2=lambda_k2,
        subln_g=subln_g, ffn_norm_g=ffn_norm_g, final_norm_g=final_norm_g,
        pool_w=pool_w.astype(BF16), w_qkv=w_qkv.astype(BF16), w_o=w_o.astype(BF16),
        w_gate=w_gate.astype(BF16), w_up=w_up.astype(BF16), w_down=w_down.astype(BF16),
    )
    return (_encode(x_prompt, p), _encode(x_sample, p))
```

```python
import functools
import math

import jax
import jax.numpy as jnp
from jax import lax
from jax.experimental import pallas as pl
from jax.experimental.pallas import tpu as pltpu

D_MODEL = 2048
N_META = 16
N_POOL_GROUPS = 4
POOL_WINDOWS = (2, 4, 8, 16)
POOL_GROUP_DIM = D_MODEL // N_POOL_GROUPS
N_HEADS = 8
HEAD_DIM = 128
V_HEAD_DIM = 2 * HEAD_DIM
D_FF = 5632
RMS_EPS = 1e-6

F32 = jnp.float32
BF16 = jnp.bfloat16

LANES = 128
V7X_VMEM_BYTES = 64 * 1024 * 1024
V7X_VMEM_COMPILER_RESERVE = 8 * 1024 * 1024

SEQ_TILE = 512
FRONT_PAD = SEQ_TILE - N_META
TOKEN_TILE = 512
WIDE_TOKEN_TILE = 1024
FF_TILE = 512
POOL_HALO = 8
WIDEST_KEY_BLOCK = 16
QUERY_TILES_PER_STEP = 4

LOG2_E = math.log2(math.e)
QK_LOG2_SCALE = HEAD_DIM ** -0.5 * LOG2_E
SKIP_LOG2_THRESHOLD = 90.0 * LOG2_E
MASKED_LOGIT = -1e30
MIN_BOUNDED_ROW_SUM = 2.0 ** -60
BOUND_SLACK = 1.0 + 2.0 ** -6


def _vmem_limit(block_bytes):
    return min(V7X_VMEM_BYTES - V7X_VMEM_COMPILER_RESERVE,
               int(block_bytes) + V7X_VMEM_COMPILER_RESERVE)


def _rmsnorm(x, g):
    ms = jnp.mean(x * x, axis=-1, keepdims=True)
    return x * lax.rsqrt(ms + RMS_EPS) * g


def _pool_kernel(cur_ref, prev_ref, next_ref, front_ref, g_ref, w_ref, scale_ref, out_ref, *,
                 n_pos):
    i = pl.program_id(1)
    n_tiles = pl.num_programs(1)
    tile = out_ref.shape[1]
    x = jnp.where(i == 0, front_ref[...], cur_ref[0])
    front_tail = front_ref[tile - POOL_HALO:, :]
    halo_lo = jnp.where(i >= 2, prev_ref[0], jnp.where(i == 1, front_tail, 0.0))
    halo_hi = jnp.where(i < n_tiles - 1, next_ref[0], 0.0)
    ext = jnp.concatenate([halo_lo, x, halo_hi], axis=0)
    hn = _rmsnorm(ext, g_ref[...])
    n_ext = ext.shape[0]

    def shifted(a, d):
        return pltpu.roll(a, (-d) % n_ext, axis=0)

    pos = i * tile - FRONT_PAD + lax.broadcasted_iota(jnp.int32, (tile, 1), 0)
    valid = pos >= 0
    for gi, w in enumerate(POOL_WINDOWS):
        cols = slice(gi * POOL_GROUP_DIM, (gi + 1) * POOL_GROUP_DIM)
        e = hn[:, cols]
        s = shifted(e, -1) + e
        reach = 1
        while 2 * reach < w:
            s = shifted(s, -reach) + shifted(s, reach)
            reach *= 2
        win = s[POOL_HALO:POOL_HALO + tile]
        cnt = jnp.minimum(pos + w // 2, n_pos) - jnp.maximum(pos - w // 2, 0)
        cnt = jnp.maximum(cnt, 1).astype(F32)
        pooled = win / cnt - e[POOL_HALO:POOL_HALO + tile]
        y = jnp.dot(pooled.astype(BF16), w_ref[gi], preferred_element_type=F32)
        y = y * scale_ref[:, cols]
        out_ref[0, :, cols] = jnp.where(valid, x[:, cols] + y, 0.0)


def _pool_layer(x, front, g, w_bf16, scale):
    bsz, s, d = x.shape
    tile = SEQ_TILE
    nt = s // tile + 1
    hb = tile // POOL_HALO
    n_halo_blocks = s // POOL_HALO
    blk = tile * d * 4
    return pl.pallas_call(
        functools.partial(_pool_kernel, n_pos=s + N_META),
        out_shape=jax.ShapeDtypeStruct((bsz, nt * tile, d), F32),
        grid=(bsz, nt),
        in_specs=[
            pl.BlockSpec((1, tile, d), lambda b, i: (b, jnp.maximum(i - 1, 0), 0)),
            pl.BlockSpec((1, POOL_HALO, d),
                         lambda b, i: (b, jnp.maximum((i - 1) * hb - 1, 0), 0)),
            pl.BlockSpec((1, POOL_HALO, d),
                         lambda b, i: (b, jnp.minimum(i * hb, n_halo_blocks - 1), 0)),
            pl.BlockSpec((tile, d), lambda b, i: (0, 0)),
            pl.BlockSpec((1, d), lambda b, i: (0, 0)),
            pl.BlockSpec((N_POOL_GROUPS, POOL_GROUP_DIM, POOL_GROUP_DIM), lambda b, i: (0, 0, 0)),
            pl.BlockSpec((1, d), lambda b, i: (0, 0)),
        ],
        out_specs=pl.BlockSpec((1, tile, d), lambda b, i: (b, i, 0)),
        compiler_params=pltpu.CompilerParams(
            dimension_semantics=("parallel", "arbitrary"),
            vmem_limit_bytes=_vmem_limit(6 * blk + 4 * w_bf16.size + 4 * blk)),
        name="pool_mixer",
    )(x, x, x, front, g.reshape(1, d), w_bf16, scale.reshape(1, d))


def _ffn_kernel(x_ref, g_ref, wg_ref, wu_ref, wd_ref, gf_ref, o_ref, hn_ref, *, final_norm):
    c = pl.program_id(1)

    def ffn_chunk(hn):
        gate = jnp.dot(hn, wg_ref[...], preferred_element_type=F32)
        up = jnp.dot(hn, wu_ref[...], preferred_element_type=F32)
        act = (gate * jax.nn.sigmoid(gate) * up).astype(BF16)
        return jnp.dot(act, wd_ref[...], preferred_element_type=F32)

    @pl.when(c == 0)
    def _():
        x = x_ref[...]
        hn = _rmsnorm(x, g_ref[...]).astype(BF16)
        hn_ref[...] = hn
        o_ref[...] = x + ffn_chunk(hn)

    @pl.when(c > 0)
    def _():
        o_ref[...] += ffn_chunk(hn_ref[...])

    if final_norm:
        @pl.when(c == pl.num_programs(1) - 1)
        def _():
            o_ref[...] = _rmsnorm(o_ref[...], gf_ref[...])


def _ffn_layer(x, g, wg, wu, wd, gf, *, layer, final_norm):
    n, d = x.shape
    tm, tf = WIDE_TOKEN_TILE, FF_TILE
    dff = wg.shape[2]
    blocks = 4 * tm * d * 4 + tm * d * 2 + 2 * 3 * d * tf * 2 + 3 * tm * tf * 4
    return pl.pallas_call(
        functools.partial(_ffn_kernel, final_norm=final_norm),
        out_shape=jax.ShapeDtypeStruct((n, d), F32),
        grid=(n // tm, dff // tf),
        in_specs=[
            pl.BlockSpec((tm, d), lambda i, c: (i, 0)),
            pl.BlockSpec((1, d), lambda i, c: (0, 0)),
            pl.BlockSpec((None, d, tf), lambda i, c: (layer, 0, c)),
            pl.BlockSpec((None, d, tf), lambda i, c: (layer, 0, c)),
            pl.BlockSpec((None, tf, d), lambda i, c: (layer, c, 0)),
            pl.BlockSpec((1, d), lambda i, c: (0, 0)),
        ],
        out_specs=pl.BlockSpec((tm, d), lambda i, c: (i, 0)),
        scratch_shapes=[pltpu.VMEM((tm, d), BF16)],
        compiler_params=pltpu.CompilerParams(
            dimension_semantics=("parallel", "arbitrary"),
            vmem_limit_bytes=_vmem_limit(blocks)),
        name="swiglu_ffn_final" if final_norm else "swiglu_ffn",
    )(x, g.reshape(1, d), wg, wu, wd, gf.reshape(1, d))


N_STAT_ROWS = 6 * N_HEADS


def _qkv_kernel(x_ref, g_ref, w_ref, q_ref, k_ref, v_ref, st_ref):
    hn = _rmsnorm(x_ref[...], g_ref[...]).astype(BF16)

    def put(row, value):
        st_ref[0, row:row + 1, :] = jnp.broadcast_to(value, (1, LANES))

    for h in range(N_HEADS):
        qk = []
        for part, ref in ((0, q_ref), (1, k_ref)):
            c0 = part * D_MODEL + h * V_HEAD_DIM
            r = jnp.dot(hn, w_ref[:, c0:c0 + V_HEAD_DIM], preferred_element_type=F32)
            if part == 0:
                r = r * QK_LOG2_SCALE
            qk.append(r)
            for m in range(2):
                rm = r[:, m * HEAD_DIM:(m + 1) * HEAD_DIM]
                ref[0, h, m] = rm.astype(BF16)
                put((2 * h + part) * 2 + m,
                    jnp.max(jnp.sum(rm * rm, axis=-1, keepdims=True), axis=0, keepdims=True))
        for m in range(2):
            sl = slice(m * HEAD_DIM, (m + 1) * HEAD_DIM)
            self_logit = jnp.sum(qk[0][:, sl] * qk[1][:, sl], axis=-1, keepdims=True)
            put(4 * N_HEADS + 2 * h + m, jnp.min(self_logit, axis=0, keepdims=True))
        c0 = 2 * D_MODEL + h * V_HEAD_DIM
        v_ref[0, h] = jnp.dot(hn, w_ref[:, c0:c0 + V_HEAD_DIM],
                              preferred_element_type=F32).astype(BF16)


def _qkv_layer(x, g, w_bf16, bsz, lp):
    n, d = x.shape
    tm = TOKEN_TILE
    nt = lp // tm
    q_shape = jax.ShapeDtypeStruct((bsz, N_HEADS, 2, lp - tm, HEAD_DIM), BF16)
    k_shape = jax.ShapeDtypeStruct((bsz, N_HEADS, 2, lp, HEAD_DIM), BF16)
    v_shape = jax.ShapeDtypeStruct((bsz, N_HEADS, lp, V_HEAD_DIM), BF16)
    sq_shape = jax.ShapeDtypeStruct((n // tm, N_STAT_ROWS, LANES), F32)
    blocks = 2 * tm * d * 4 + w_bf16.size * 2 + 2 * 3 * tm * d * 2 + tm * d * 2
    return pl.pallas_call(
        _qkv_kernel,
        out_shape=(q_shape, k_shape, v_shape, sq_shape),
        grid=(n // tm,),
        in_specs=[
            pl.BlockSpec((tm, d), lambda i: (i, 0)),
            pl.BlockSpec((1, d), lambda i: (0, 0)),
            pl.BlockSpec((d, 3 * d), lambda i: (0, 0), pipeline_mode=pl.Buffered(1)),
        ],
        out_specs=(
            pl.BlockSpec((1, N_HEADS, 2, tm, HEAD_DIM),
                         lambda i: (i // nt, 0, 0, jnp.maximum(i % nt - 1, 0), 0)),
            pl.BlockSpec((1, N_HEADS, 2, tm, HEAD_DIM), lambda i: (i // nt, 0, 0, i % nt, 0)),
            pl.BlockSpec((1, N_HEADS, tm, V_HEAD_DIM), lambda i: (i // nt, 0, i % nt, 0)),
            pl.BlockSpec((1, N_STAT_ROWS, LANES), lambda i: (i, 0, 0)),
        ),
        compiler_params=pltpu.CompilerParams(
            dimension_semantics=("arbitrary",),
            vmem_limit_bytes=_vmem_limit(blocks)),
        name="qkv_proj",
    )(x, g.reshape(1, d), w_bf16)


BIAS_LEFT, BIAS_DIAG, BIAS_RIGHT, BIAS_FRONT = 0, 1, 2, 3
N_BIAS_CLASSES = 4
_NT_DIMS = (((1,), (1,)), ((), ()))


def _attn_kernel(slope_ref, bound_ref, reach_ref, q_ref, k_ref, v_ref, lq1_ref, lk1_ref, lq2_ref,
                 lk2_ref, subg_ref, o_ref, bias_ref, l_ref, acc_ref, ls_ref, m_ref, lrun_ref, *,
                 lambda_init):
    b, h, step = pl.program_id(0), pl.program_id(1), pl.program_id(2)
    t = bias_ref.shape[1]
    n_sub = q_ref.shape[3] // t
    n_rows = k_ref.shape[3]
    n_chunks = t // LANES
    log2_t = t.bit_length() - 1
    assert t == 1 << log2_t
    sigma = slope_ref[h] * LOG2_E
    bounds = [bound_ref[(b * N_HEADS + h) * 2 + m] for m in range(2)]
    reach = reach_ref[b * N_HEADS + h]

    def key_range(qi):
        lo = lax.shift_right_logical(jnp.maximum(qi * t - reach, 0), log2_t)
        hi = lax.shift_right_logical(jnp.minimum(qi * t + (t - 1) + reach, n_rows - 1), log2_t)
        return lo, hi + 1

    @pl.when(step == 0)
    def _():
        r = lax.broadcasted_iota(jnp.int32, (t, t), 0)
        c = lax.broadcasted_iota(jnp.int32, (t, t), 1)
        left = -sigma * (r - c).astype(F32)
        bias_ref[BIAS_LEFT] = left
        bias_ref[BIAS_DIAG] = -jnp.abs(left)
        bias_ref[BIAS_RIGHT] = -left
        bias_ref[BIAS_FRONT] = jnp.where(c >= FRONT_PAD, left, MASKED_LOGIT)

    def lane_sums(x):
        ones = jnp.ones((x.shape[1], LANES), BF16)
        hi_part = x.astype(BF16)
        lo_part = (x - hi_part.astype(F32)).astype(BF16)
        return (jnp.dot(hi_part, ones, preferred_element_type=F32)
                + jnp.dot(lo_part, ones, preferred_element_type=F32))

    l_ref[...] = jnp.zeros(l_ref.shape, F32)
    acc_ref[...] = jnp.zeros(acc_ref.shape, F32)

    def query_rows(sub, m):
        return q_ref[0, 0, m, pl.ds(pl.multiple_of(sub * t, t), t), :]

    def tile_class(qi, j):
        side = jnp.where(j < qi, BIAS_LEFT, jnp.where(j == qi, BIAS_DIAG, BIAS_RIGHT))
        return jnp.where(j == 0, BIAS_FRONT, side)

    def tile_offset(qi, j):
        gap = jnp.full((1, LANES), jnp.abs(qi - j) * t, jnp.int32).astype(F32)
        return gap * (-sigma)

    def bounded_tile(sub, carry):
        qi = step * n_sub + sub + 1
        lo, hi = key_range(qi)

        def bounded_block(j0, n_blk):
            start = pl.multiple_of(j0 * t, t)
            width = n_blk * t
            classes = [tile_class(qi, j0 + u) for u in range(n_blk)]
            offsets = [tile_offset(qi, j0 + u) for u in range(n_blk)]
            vv = v_ref[0, 0, pl.ds(start, width), :]
            for m in range(2):
                kk = k_ref[0, 0, m, pl.ds(start, width), :]
                s = lax.dot_general(query_rows(sub, m), kk, _NT_DIMS,
                                    preferred_element_type=F32)
                row_sum = None
                ps = []
                for u in range(n_blk):
                    shift = offsets[u] - bounds[m]
                    for cc in range(n_chunks):
                        sl = slice(cc * LANES, (cc + 1) * LANES)
                        su = s[:, u * t + cc * LANES:u * t + (cc + 1) * LANES]
                        p = jnp.exp2(su + bias_ref[classes[u], :, sl] + shift)
                        row_sum = p if row_sum is None else row_sum + p
                        ps.append(p.astype(BF16))
                l_ref[sub, m] += row_sum
                acc_ref[sub, m] += jnp.dot(jnp.concatenate(ps, axis=1), vv,
                                           preferred_element_type=F32)

        n_steps = hi - lo
        log2_wide = WIDEST_KEY_BLOCK.bit_length() - 1
        n_wide = lax.shift_right_logical(n_steps, log2_wide)

        def wide_body(i, c):
            bounded_block(lo + WIDEST_KEY_BLOCK * i, WIDEST_KEY_BLOCK)
            return c

        lax.fori_loop(0, n_wide, wide_body, 0)
        done = n_wide * WIDEST_KEY_BLOCK
        for width in (WIDEST_KEY_BLOCK >> s for s in range(1, log2_wide + 1)):
            covered = done + jnp.bitwise_and(n_steps,
                                             (WIDEST_KEY_BLOCK - 1) & ~(2 * width - 1))

            @pl.when(jnp.bitwise_and(n_steps, width) != 0)
            def _(covered=covered, width=width):
                bounded_block(lo + covered, width)
        return carry

    lax.fori_loop(0, n_sub, bounded_tile, 0)

    def finalize():
        lam = (jnp.exp(jnp.sum(lq1_ref[...] * lk1_ref[...], axis=-1, keepdims=True))
               - jnp.exp(jnp.sum(lq2_ref[...] * lk2_ref[...], axis=-1, keepdims=True))
               + lambda_init)
        gain = subg_ref[...] * (1.0 - lambda_init)
        for sub in range(n_sub):
            inv = [1.0 / ls_ref[sub, m] for m in range(2)]
            o = (acc_ref[sub, 0] * jnp.concatenate([inv[0], inv[0]], axis=1)
                 - lam * (acc_ref[sub, 1] * jnp.concatenate([inv[1], inv[1]], axis=1)))
            r = lax.rsqrt(lane_sums(o * o) * (1.0 / V_HEAD_DIM) + RMS_EPS)
            o = o * jnp.concatenate([r, r], axis=1) * gain
            o_ref[0, sub * t:(sub + 1) * t, :] = o.astype(BF16)

    smallest = None
    for sub in range(n_sub):
        for m in range(2):
            ls = lane_sums(l_ref[sub, m])
            ls_ref[sub, m] = ls
            smallest = ls if smallest is None else jnp.minimum(smallest, ls)
    smallest = jnp.min(smallest)
    finalize()

    @pl.when(smallest < MIN_BOUNDED_ROW_SUM)
    def _():
        def online_tile(sub, carry):
            qi = step * n_sub + sub + 1
            lo, hi = key_range(qi)
            m_ref[...] = jnp.full(m_ref.shape, MASKED_LOGIT, F32)
            lrun_ref[...] = jnp.zeros(lrun_ref.shape, F32)
            acc_ref[sub] = jnp.zeros(acc_ref.shape[1:], F32)

            def online_step(j, c):
                start = pl.multiple_of(j * t, t)
                cls = tile_class(qi, j)
                off = tile_offset(qi, j)[:, :1]
                vv = v_ref[0, 0, pl.ds(start, t), :]
                for m in range(2):
                    kk = k_ref[0, 0, m, pl.ds(start, t), :]
                    s = lax.dot_general(query_rows(sub, m), kk, _NT_DIMS,
                                        preferred_element_type=F32)
                    s = s + bias_ref[cls] + off
                    m_prev = m_ref[m]
                    m_new = jnp.maximum(m_prev, jnp.max(s, axis=-1, keepdims=True))
                    alpha = jnp.exp2(m_prev - m_new)
                    p = jnp.exp2(s - m_new)
                    lrun_ref[m] = alpha * lrun_ref[m] + jnp.sum(p, axis=-1, keepdims=True)
                    acc_ref[sub, m] = alpha * acc_ref[sub, m] + jnp.dot(
                        p.astype(BF16), vv, preferred_element_type=F32)
                    m_ref[m] = m_new
                return c

            lax.fori_loop(lo, hi, online_step, 0)
            for m in range(2):
                ls_ref[sub, m] = jnp.broadcast_to(lrun_ref[m], (t, LANES))
            return carry

        lax.fori_loop(0, n_sub, online_tile, 0)
        finalize()


def _attention_bounds(stats, slopes, bsz, lp):
    stats = stats[:, :, 0].reshape(bsz, -1, N_STAT_ROWS)
    norms = jnp.sqrt(jnp.max(stats[:, :, :4 * N_HEADS], axis=1)).reshape(bsz, N_HEADS, 2, 2)
    lowest_self = jnp.min(stats[:, :, 4 * N_HEADS:], axis=1).reshape(bsz, N_HEADS, 2)
    product = norms[:, :, 0] * norms[:, :, 1] * BOUND_SLACK
    spread = jnp.max(product * BOUND_SLACK - lowest_self, axis=-1) + 1.0
    reach = (SKIP_LOG2_THRESHOLD + spread) / (slopes * LOG2_E)[None, :]
    reach = jnp.ceil(jnp.minimum(reach, float(lp))).astype(jnp.int32) + 1
    return product.reshape(-1), reach.reshape(-1)


def _attention_layer(q, k, v, slopes, bound, reach, lq1, lk1, lq2, lk2, subg, lambda_init):
    bsz, n_heads, _, s, dh = q.shape
    lp = k.shape[3]
    tq, n_sub = SEQ_TILE, QUERY_TILES_PER_STEP
    rows = tq * n_sub
    assert s % rows == 0
    dv = v.shape[-1]
    kv_bytes = 2 * (2 * lp * dh * 2 + lp * dv * 2)
    scratch = (N_BIAS_CLASSES * tq * tq + n_sub * (4 * tq * LANES + 2 * tq * dv)) * 4
    blocks = kv_bytes + 2 * 2 * rows * dh * 2 + 2 * rows * dv * 2 + scratch
    vec = lambda b, h, i, *_: (0, 0)
    grid_spec = pltpu.PrefetchScalarGridSpec(
        num_scalar_prefetch=3,
        grid=(bsz, n_heads, s // rows),
        in_specs=[
            pl.BlockSpec((1, 1, 2, rows, dh), lambda b, h, i, *_: (b, h, 0, i, 0)),
            pl.BlockSpec((1, 1, 2, lp, dh), lambda b, h, i, *_: (b, h, 0, 0, 0)),
            pl.BlockSpec((1, 1, lp, dv), lambda b, h, i, *_: (b, h, 0, 0)),
            pl.BlockSpec((1, dh), vec), pl.BlockSpec((1, dh), vec),
            pl.BlockSpec((1, dh), vec), pl.BlockSpec((1, dh), vec),
            pl.BlockSpec((1, dv), vec),
        ],
        out_specs=pl.BlockSpec((1, rows, dv), lambda b, h, i, *_: (b, i, h)),
        scratch_shapes=[
            pltpu.VMEM((N_BIAS_CLASSES, tq, tq), F32),
            pltpu.VMEM((n_sub, 2, tq, LANES), F32),
            pltpu.VMEM((n_sub, 2, tq, dv), F32),
            pltpu.VMEM((n_sub, 2, tq, LANES), F32),
            pltpu.VMEM((2, tq, 1), F32),
            pltpu.VMEM((2, tq, 1), F32),
        ],
    )
    return pl.pallas_call(
        functools.partial(_attn_kernel, lambda_init=lambda_init),
        out_shape=jax.ShapeDtypeStruct((bsz, s, n_heads * dv), BF16),
        grid_spec=grid_spec,
        compiler_params=pltpu.CompilerParams(
            dimension_semantics=("arbitrary", "arbitrary", "arbitrary"),
            vmem_limit_bytes=_vmem_limit(blocks + 6 * 4 * tq * tq * 4)),
        name="diff_attention",
    )(slopes, bound, reach, q, k, v, lq1.reshape(1, dh), lk1.reshape(1, dh),
      lq2.reshape(1, dh), lk2.reshape(1, dh), subg.reshape(1, dv))


def _oproj_kernel(h_ref, a_ref, w_ref, out_ref):
    out_ref[...] = h_ref[...] + jnp.dot(a_ref[...], w_ref[...], preferred_element_type=F32)


def _oproj_layer(h, attn, w_bf16, h_tile_of):
    n, d = attn.shape
    tm = TOKEN_TILE
    blocks = 4 * tm * d * 4 + 2 * tm * d * 2 + 2 * d * d * 2 + tm * d * 4
    return pl.pallas_call(
        _oproj_kernel,
        out_shape=jax.ShapeDtypeStruct((n, d), F32),
        grid=(n // tm,),
        in_specs=[
            pl.BlockSpec((tm, d), lambda i: (h_tile_of(i), 0)),
            pl.BlockSpec((tm, d), lambda i: (i, 0)),
            pl.BlockSpec((d, d), lambda i: (0, 0)),
        ],
        out_specs=pl.BlockSpec((tm, d), lambda i: (i, 0)),
        compiler_params=pltpu.CompilerParams(
            dimension_semantics=("parallel",),
            vmem_limit_bytes=_vmem_limit(blocks)),
        name="attn_out_proj",
    )(h, attn, w_bf16)


def _encode(x, p):
    bsz, s, d = x.shape
    assert s % SEQ_TILE == 0 and SEQ_TILE == TOKEN_TILE
    lp = s + SEQ_TILE
    assert (bsz * lp) % WIDE_TOKEN_TILE == 0 and (bsz * s) % WIDE_TOKEN_TILE == 0
    tiles_in, tiles_out = lp // TOKEN_TILE, s // TOKEN_TILE
    front = jnp.concatenate([jnp.zeros((FRONT_PAD, d), x.dtype), p["meta_tokens"]], axis=0)

    h = _pool_layer(x, front, p["mixer_norm_g"][0], p["pool_w"][0], p["pool_scale"][0])
    h = h.reshape(bsz * lp, d)
    h = _ffn_layer(h, p["ffn_norm_g"][0], p["w_gate"], p["w_up"], p["w_down"],
                   p["final_norm_g"], layer=0, final_norm=False)

    lambda_init = 0.8 - 0.6 * math.exp(-0.3 * 1)
    slopes = 2.0 ** (-8.0 * (jnp.arange(N_HEADS, dtype=F32) + 1.0) / N_HEADS)
    q, k, v, sq = _qkv_layer(h, p["mixer_norm_g"][1], p["w_qkv"][0], bsz, lp)
    bound, reach = _attention_bounds(sq, slopes, bsz, lp)
    o = _attention_layer(q, k, v, slopes, bound, reach, p["lambda_q1"][0], p["lambda_k1"][0],
                         p["lambda_q2"][0], p["lambda_k2"][0], p["subln_g"][0], lambda_init)
    h = _oproj_layer(h, o.reshape(bsz * s, d), p["w_o"][0],
                     h_tile_of=lambda i: (i // tiles_out) * tiles_in + i % tiles_out + 1)
    y = _ffn_layer(h, p["ffn_norm_g"][1], p["w_gate"], p["w_up"], p["w_down"],
                   p["final_norm_g"], layer=1, final_norm=True)
    return y.reshape(bsz, s, d)


def kernel(x_prompt, x_sample, meta_tokens, mixer_norm_g, pool_w, pool_scale, w_qkv, lambda_q1,
           lambda_k1, lambda_q2, lambda_k2, subln_g, w_o, ffn_norm_g, w_gate, w_up, w_down,
           final_norm_g):
    p = dict(
        meta_tokens=meta_tokens, mixer_norm_g=mixer_norm_g, pool_scale=pool_scale,
        lambda_q1=lambda_q1, lambda_k1=lambda_k1, lambda_q2=lambda_q2, lambda_k2=lambda_k2,
        subln_g=subln_g, ffn_norm_g=ffn_norm_g, final_norm_g=final_norm_g,
        pool_w=pool_w.astype(BF16), w_qkv=w_qkv.astype(BF16), w_o=w_o.astype(BF16),
        w_gate=w_gate.astype(BF16), w_up=w_up.astype(BF16), w_down=w_down.astype(BF16),
    )
    return (_encode(x_prompt, p), _encode(x_sample, p))
```

```python
import functools
import math

import jax
import jax.numpy as jnp
from jax import lax
from jax.experimental import pallas as pl
from jax.experimental.pallas import tpu as pltpu

D_MODEL = 2048
N_META = 16
N_POOL_GROUPS = 4
POOL_WINDOWS = (2, 4, 8, 16)
POOL_GROUP_DIM = D_MODEL // N_POOL_GROUPS
N_HEADS = 8
HEAD_DIM = 128
V_HEAD_DIM = 2 * HEAD_DIM
RMS_EPS = 1e-6

F32 = jnp.float32
BF16 = jnp.bfloat16

LANES = 128
V7X_VMEM_BYTES = 64 * 1024 * 1024
V7X_VMEM_COMPILER_RESERVE = 8 * 1024 * 1024
V7X_VMEM_KEPT_FREE = 4 * 1024 * 1024

SEQ_TILE = 512
FRONT_PAD = SEQ_TILE - N_META
TOKEN_TILE = 512
WIDE_TOKEN_TILE = 1024
FF_TILE = 512
POOL_HALO = 8
WIDEST_KEY_BLOCK = 16
QUERY_TILES_PER_STEP = 4

LOG2_E = math.log2(math.e)
QK_LOG2_SCALE = HEAD_DIM ** -0.5 * LOG2_E
SKIP_LOG2_THRESHOLD = 90.0 * LOG2_E
MASKED_LOGIT = -1e30
MIN_BOUNDED_ROW_SUM = 2.0 ** -60
BOUND_SLACK = 1.0 + 2.0 ** -6


def _vmem_limit(block_bytes):
    return min(V7X_VMEM_BYTES - V7X_VMEM_KEPT_FREE,
               int(block_bytes) + V7X_VMEM_COMPILER_RESERVE)


def _rmsnorm(x, g):
    ms = jnp.mean(x * x, axis=-1, keepdims=True)
    return x * lax.rsqrt(ms + RMS_EPS) * g


def _pool_kernel(cur_ref, prev_ref, next_ref, front_ref, g_ref, w_ref, scale_ref, out_ref, *,
                 n_pos):
    i = pl.program_id(1)
    n_tiles = pl.num_programs(1)
    tile = out_ref.shape[1]
    x = jnp.where(i == 0, front_ref[...], cur_ref[0])
    front_tail = front_ref[tile - POOL_HALO:, :]
    halo_lo = jnp.where(i >= 2, prev_ref[0], jnp.where(i == 1, front_tail, 0.0))
    halo_hi = jnp.where(i < n_tiles - 1, next_ref[0], 0.0)
    ext = jnp.concatenate([halo_lo, x, halo_hi], axis=0)
    hn = _rmsnorm(ext, g_ref[...])
    n_ext = ext.shape[0]

    def shifted(a, d):
        return pltpu.roll(a, (-d) % n_ext, axis=0)

    pos = i * tile - FRONT_PAD + lax.broadcasted_iota(jnp.int32, (tile, 1), 0)
    valid = pos >= 0
    for gi, w in enumerate(POOL_WINDOWS):
        cols = slice(gi * POOL_GROUP_DIM, (gi + 1) * POOL_GROUP_DIM)
        e = hn[:, cols]
        f = e
        span = 1
        while 2 * span < w:
            f = f + shifted(f, span)
            span *= 2
        if span == POOL_HALO:
            win = f[:tile] + f[POOL_HALO:POOL_HALO + tile]
        else:
            win = (shifted(f, -span) + f)[POOL_HALO:POOL_HALO + tile]
        cnt = jnp.minimum(pos + w // 2, n_pos) - jnp.maximum(pos - w // 2, 0)
        cnt = jnp.maximum(cnt, 1).astype(F32)
        pooled = win / cnt - e[POOL_HALO:POOL_HALO + tile]
        y = jnp.dot(pooled.astype(BF16), w_ref[gi], preferred_element_type=F32)
        y = y * scale_ref[:, cols]
        out_ref[0, :, cols] = jnp.where(valid, x[:, cols] + y, 0.0)


def _pool_layer(x, front, g, w_bf16, scale):
    bsz, s, d = x.shape
    tile = SEQ_TILE
    nt = s // tile + 1
    hb = tile // POOL_HALO
    n_halo_blocks = s // POOL_HALO
    blk = tile * d * 4
    blocks = 6 * blk + 2 * w_bf16.size * 2 + 4 * blk
    return pl.pallas_call(
        functools.partial(_pool_kernel, n_pos=s + N_META),
        out_shape=jax.ShapeDtypeStruct((bsz, nt * tile, d), F32),
        grid=(bsz, nt),
        in_specs=[
            pl.BlockSpec((1, tile, d), lambda b, i: (b, jnp.maximum(i - 1, 0), 0)),
            pl.BlockSpec((1, POOL_HALO, d),
                         lambda b, i: (b, jnp.maximum((i - 1) * hb - 1, 0), 0)),
            pl.BlockSpec((1, POOL_HALO, d),
                         lambda b, i: (b, jnp.minimum(i * hb, n_halo_blocks - 1), 0)),
            pl.BlockSpec((tile, d), lambda b, i: (0, 0)),
            pl.BlockSpec((1, d), lambda b, i: (0, 0)),
            pl.BlockSpec((N_POOL_GROUPS, POOL_GROUP_DIM, POOL_GROUP_DIM), lambda b, i: (0, 0, 0)),
            pl.BlockSpec((1, d), lambda b, i: (0, 0)),
        ],
        out_specs=pl.BlockSpec((1, tile, d), lambda b, i: (b, i, 0)),
        compiler_params=pltpu.CompilerParams(
            dimension_semantics=("parallel", "arbitrary"),
            vmem_limit_bytes=_vmem_limit(blocks)),
        name="pool_mixer",
    )(x, x, x, front, g.reshape(1, d), w_bf16, scale.reshape(1, d))


def _ffn_kernel(x_ref, g_ref, wg_hbm, wu_hbm, wd_hbm, gf_ref, o_ref, hn_ref, wg_buf, wu_buf,
                wd_buf, sem, *, layer, final_norm):
    i = pl.program_id(0)
    tf = wg_buf.shape[2]
    n_chunks = wg_hbm.shape[2] // tf
    first = lax.rem(i * n_chunks, 2)
    other = 1 - first

    def chunk_copies(c, slot):
        col = pl.multiple_of(c * tf, tf)
        return (
            pltpu.make_async_copy(wg_hbm.at[layer, :, pl.ds(col, tf)], wg_buf.at[slot],
                                  sem.at[0, slot]),
            pltpu.make_async_copy(wu_hbm.at[layer, :, pl.ds(col, tf)], wu_buf.at[slot],
                                  sem.at[1, slot]),
            pltpu.make_async_copy(wd_hbm.at[layer, pl.ds(col, tf), :], wd_buf.at[slot],
                                  sem.at[2, slot]),
        )

    def start(c, slot):
        for copy in chunk_copies(c, slot):
            copy.start()

    def wait(c, slot):
        for copy in chunk_copies(c, slot):
            copy.wait()

    def add_chunk(slot, base_ref):
        hn = hn_ref[...]
        gate = jnp.dot(hn, wg_buf[slot], preferred_element_type=F32)
        up = jnp.dot(hn, wu_buf[slot], preferred_element_type=F32)
        act = (gate * jax.nn.sigmoid(gate) * up).astype(BF16)
        for col in range(0, o_ref.shape[1], tf):
            cols = slice(col, col + tf)
            o_ref[:, cols] = base_ref[:, cols] + jnp.dot(act, wd_buf[slot, :, cols],
                                                        preferred_element_type=F32)

    @pl.when(i == 0)
    def _():
        start(0, first)

    hn_ref[...] = _rmsnorm(x_ref[...], g_ref[...]).astype(BF16)
    wait(0, first)
    start(1, other)
    add_chunk(first, x_ref)

    def later_chunk(c, carry):
        slot = lax.rem(first + c, 2)
        following = jnp.where(c + 1 < n_chunks, c + 1, 0)
        wait(c, slot)
        start(following, 1 - slot)
        add_chunk(slot, o_ref)
        return carry

    lax.fori_loop(1, n_chunks, later_chunk, 0)

    @pl.when(i == pl.num_programs(0) - 1)
    def _():
        wait(0, lax.rem(first + n_chunks, 2))

    if final_norm:
        o_ref[...] = _rmsnorm(o_ref[...], gf_ref[...])


def _ffn_layer(x, g, wg, wu, wd, gf, *, layer, final_norm, tm=WIDE_TOKEN_TILE, tf=FF_TILE):
    n, d = x.shape
    blocks = 4 * tm * d * 4 + tm * d * 2 + 2 * 3 * d * tf * 2 + 3 * tm * tf * 4
    hbm = pl.BlockSpec(memory_space=pl.ANY)
    return pl.pallas_call(
        functools.partial(_ffn_kernel, layer=layer, final_norm=final_norm),
        out_shape=jax.ShapeDtypeStruct((n, d), F32),
        grid=(n // tm,),
        in_specs=[
            pl.BlockSpec((tm, d), lambda i: (i, 0)),
            pl.BlockSpec((1, d), lambda i: (0, 0)),
            hbm, hbm, hbm,
            pl.BlockSpec((1, d), lambda i: (0, 0)),
        ],
        out_specs=pl.BlockSpec((tm, d), lambda i: (i, 0)),
        scratch_shapes=[
            pltpu.VMEM((tm, d), BF16),
            pltpu.VMEM((2, d, tf), BF16),
            pltpu.VMEM((2, d, tf), BF16),
            pltpu.VMEM((2, tf, d), BF16),
            pltpu.SemaphoreType.DMA((3, 2)),
        ],
        compiler_params=pltpu.CompilerParams(
            dimension_semantics=("arbitrary",),
            vmem_limit_bytes=_vmem_limit(blocks)),
        name="swiglu_ffn_final" if final_norm else "swiglu_ffn",
    )(x, g.reshape(1, d), wg, wu, wd, gf.reshape(1, d))


N_STAT_ROWS = 6 * N_HEADS


def _qkv_kernel(x_ref, g_ref, w_ref, q_ref, k_ref, v_ref, st_ref):
    hn = _rmsnorm(x_ref[...], g_ref[...]).astype(BF16)

    def put(row, value):
        st_ref[0, row:row + 1, :] = jnp.broadcast_to(value, (1, LANES))

    for h in range(N_HEADS):
        qk = []
        for part, ref in ((0, q_ref), (1, k_ref)):
            c0 = part * D_MODEL + h * V_HEAD_DIM
            r = jnp.dot(hn, w_ref[:, c0:c0 + V_HEAD_DIM], preferred_element_type=F32)
            if part == 0:
                r = r * QK_LOG2_SCALE
            qk.append(r)
            for m in range(2):
                rm = r[:, m * HEAD_DIM:(m + 1) * HEAD_DIM]
                ref[0, h, m] = rm.astype(BF16)
                put((2 * h + part) * 2 + m,
                    jnp.max(jnp.sum(rm * rm, axis=-1, keepdims=True), axis=0, keepdims=True))
        for m in range(2):
            sl = slice(m * HEAD_DIM, (m + 1) * HEAD_DIM)
            self_logit = jnp.sum(qk[0][:, sl] * qk[1][:, sl], axis=-1, keepdims=True)
            put(4 * N_HEADS + 2 * h + m, jnp.min(self_logit, axis=0, keepdims=True))
        c0 = 2 * D_MODEL + h * V_HEAD_DIM
        v_ref[0, h] = jnp.dot(hn, w_ref[:, c0:c0 + V_HEAD_DIM],
                              preferred_element_type=F32).astype(BF16)


def _qkv_layer(x, g, w_bf16, bsz, lp):
    n, d = x.shape
    tm = TOKEN_TILE
    nt = lp // tm
    q_shape = jax.ShapeDtypeStruct((bsz, N_HEADS, 2, lp - tm, HEAD_DIM), BF16)
    k_shape = jax.ShapeDtypeStruct((bsz, N_HEADS, 2, lp, HEAD_DIM), BF16)
    v_shape = jax.ShapeDtypeStruct((bsz, N_HEADS, lp, V_HEAD_DIM), BF16)
    sq_shape = jax.ShapeDtypeStruct((n // tm, N_STAT_ROWS, LANES), F32)
    blocks = 2 * tm * d * 4 + w_bf16.size * 2 + 2 * 3 * tm * d * 2 + tm * d * 2
    return pl.pallas_call(
        _qkv_kernel,
        out_shape=(q_shape, k_shape, v_shape, sq_shape),
        grid=(n // tm,),
        in_specs=[
            pl.BlockSpec((tm, d), lambda i: (i, 0)),
            pl.BlockSpec((1, d), lambda i: (0, 0)),
            pl.BlockSpec((d, 3 * d), lambda i: (0, 0), pipeline_mode=pl.Buffered(1)),
        ],
        out_specs=(
            pl.BlockSpec((1, N_HEADS, 2, tm, HEAD_DIM),
                         lambda i: (i // nt, 0, 0, jnp.maximum(i % nt - 1, 0), 0)),
            pl.BlockSpec((1, N_HEADS, 2, tm, HEAD_DIM), lambda i: (i // nt, 0, 0, i % nt, 0)),
            pl.BlockSpec((1, N_HEADS, tm, V_HEAD_DIM), lambda i: (i // nt, 0, i % nt, 0)),
            pl.BlockSpec((1, N_STAT_ROWS, LANES), lambda i: (i, 0, 0)),
        ),
        compiler_params=pltpu.CompilerParams(
            dimension_semantics=("arbitrary",),
            vmem_limit_bytes=_vmem_limit(blocks)),
        name="qkv_proj",
    )(x, g.reshape(1, d), w_bf16)


BIAS_LEFT, BIAS_DIAG, BIAS_RIGHT, BIAS_FRONT = 0, 1, 2, 3
N_BIAS_CLASSES = 4
_NT_DIMS = (((1,), (1,)), ((), ()))


def _attn_kernel(slope_ref, bound_ref, reach_ref, q_ref, k_ref, v_ref, lq1_ref, lk1_ref, lq2_ref,
                 lk2_ref, subg_ref, o_ref, bias_ref, l_ref, acc_ref, ls_ref, m_ref, lrun_ref, *,
                 lambda_init):
    b, h, step = pl.program_id(0), pl.program_id(1), pl.program_id(2)
    t = bias_ref.shape[1]
    n_sub = q_ref.shape[3] // t
    n_rows = k_ref.shape[3]
    n_chunks = t // LANES
    log2_t = t.bit_length() - 1
    assert t == 1 << log2_t
    sigma = slope_ref[h] * LOG2_E
    bounds = [bound_ref[(b * N_HEADS + h) * 2 + m] for m in range(2)]
    reach = reach_ref[b * N_HEADS + h]

    def key_range(qi):
        lo = lax.shift_right_logical(jnp.maximum(qi * t - reach, 0), log2_t)
        hi = lax.shift_right_logical(jnp.minimum(qi * t + (t - 1) + reach, n_rows - 1), log2_t)
        return lo, hi + 1

    @pl.when(step == 0)
    def _():
        r = lax.broadcasted_iota(jnp.int32, (t, t), 0)
        c = lax.broadcasted_iota(jnp.int32, (t, t), 1)
        left = -sigma * (r - c).astype(F32)
        bias_ref[BIAS_LEFT] = left
        bias_ref[BIAS_DIAG] = -jnp.abs(left)
        bias_ref[BIAS_RIGHT] = -left
        bias_ref[BIAS_FRONT] = jnp.where(c >= FRONT_PAD, left, MASKED_LOGIT)

    def lane_sums(x):
        ones = jnp.ones((x.shape[1], LANES), BF16)
        hi_part = x.astype(BF16)
        lo_part = (x - hi_part.astype(F32)).astype(BF16)
        return (jnp.dot(hi_part, ones, preferred_element_type=F32)
                + jnp.dot(lo_part, ones, preferred_element_type=F32))

    l_ref[...] = jnp.zeros(l_ref.shape, F32)
    acc_ref[...] = jnp.zeros(acc_ref.shape, F32)

    def query_rows(sub, m):
        return q_ref[0, 0, m, pl.ds(pl.multiple_of(sub * t, t), t), :]

    def tile_class(qi, j):
        side = jnp.where(j < qi, BIAS_LEFT, jnp.where(j == qi, BIAS_DIAG, BIAS_RIGHT))
        return jnp.where(j == 0, BIAS_FRONT, side)

    def tile_offset(qi, j):
        gap = jnp.full((1, LANES), jnp.abs(qi - j) * t, jnp.int32).astype(F32)
        return gap * (-sigma)

    def bounded_tile(sub, carry):
        qi = step * n_sub + sub + 1
        lo, hi = key_range(qi)

        def bounded_block(j0, n_blk):
            start = pl.multiple_of(j0 * t, t)
            width = n_blk * t
            classes = [tile_class(qi, j0 + u) for u in range(n_blk)]
            offsets = [tile_offset(qi, j0 + u) for u in range(n_blk)]
            vv = v_ref[0, 0, pl.ds(start, width), :]
            for m in range(2):
                kk = k_ref[0, 0, m, pl.ds(start, width), :]
                s = lax.dot_general(query_rows(sub, m), kk, _NT_DIMS,
                                    preferred_element_type=F32)
                row_sum = None
                ps = []
                for u in range(n_blk):
                    shift = offsets[u] - bounds[m]
                    for cc in range(n_chunks):
                        sl = slice(cc * LANES, (cc + 1) * LANES)
                        su = s[:, u * t + cc * LANES:u * t + (cc + 1) * LANES]
                        p = jnp.exp2(su + bias_ref[classes[u], :, sl] + shift)
                        row_sum = p if row_sum is None else row_sum + p
                        ps.append(p.astype(BF16))
                l_ref[sub, m] += row_sum
                acc_ref[sub, m] += jnp.dot(jnp.concatenate(ps, axis=1), vv,
                                           preferred_element_type=F32)

        n_steps = hi - lo
        log2_wide = WIDEST_KEY_BLOCK.bit_length() - 1
        n_wide = lax.shift_right_logical(n_steps, log2_wide)

        def wide_body(i, c):
            bounded_block(lo + WIDEST_KEY_BLOCK * i, WIDEST_KEY_BLOCK)
            return c

        lax.fori_loop(0, n_wide, wide_body, 0)
        done = n_wide * WIDEST_KEY_BLOCK
        for width in (WIDEST_KEY_BLOCK >> s for s in range(1, log2_wide + 1)):
            covered = done + jnp.bitwise_and(n_steps,
                                             (WIDEST_KEY_BLOCK - 1) & ~(2 * width - 1))

            @pl.when(jnp.bitwise_and(n_steps, width) != 0)
            def _(covered=covered, width=width):
                bounded_block(lo + covered, width)
        return carry

    lax.fori_loop(0, n_sub, bounded_tile, 0)

    def finalize():
        lam = (jnp.exp(jnp.sum(lq1_ref[...] * lk1_ref[...], axis=-1, keepdims=True))
               - jnp.exp(jnp.sum(lq2_ref[...] * lk2_ref[...], axis=-1, keepdims=True))
               + lambda_init)
        gain = subg_ref[...] * (1.0 - lambda_init)
        for sub in range(n_sub):
            inv = [1.0 / ls_ref[sub, m] for m in range(2)]
            o = (acc_ref[sub, 0] * jnp.concatenate([inv[0], inv[0]], axis=1)
                 - lam * (acc_ref[sub, 1] * jnp.concatenate([inv[1], inv[1]], axis=1)))
            r = lax.rsqrt(lane_sums(o * o) * (1.0 / V_HEAD_DIM) + RMS_EPS)
            o = o * jnp.concatenate([r, r], axis=1) * gain
            o_ref[0, sub * t:(sub + 1) * t, :] = o.astype(BF16)

    smallest = None
    for sub in range(n_sub):
        for m in range(2):
            ls = lane_sums(l_ref[sub, m])
            ls_ref[sub, m] = ls
            smallest = ls if smallest is None else jnp.minimum(smallest, ls)
    smallest = jnp.min(smallest)
    finalize()

    @pl.when(smallest < MIN_BOUNDED_ROW_SUM)
    def _():
        def online_tile(sub, carry):
            qi = step * n_sub + sub + 1
            lo, hi = key_range(qi)
            m_ref[...] = jnp.full(m_ref.shape, MASKED_LOGIT, F32)
            lrun_ref[...] = jnp.zeros(lrun_ref.shape, F32)
            acc_ref[sub] = jnp.zeros(acc_ref.shape[1:], F32)

            def online_step(j, c):
                start = pl.multiple_of(j * t, t)
                cls = tile_class(qi, j)
                off = tile_offset(qi, j)[:, :1]
                vv = v_ref[0, 0, pl.ds(start, t), :]
                for m in range(2):
                    kk = k_ref[0, 0, m, pl.ds(start, t), :]
                    s = lax.dot_general(query_rows(sub, m), kk, _NT_DIMS,
                                        preferred_element_type=F32)
                    s = s + bias_ref[cls] + off
                    m_prev = m_ref[m]
                    m_new = jnp.maximum(m_prev, jnp.max(s, axis=-1, keepdims=True))
                    alpha = jnp.exp2(m_prev - m_new)
                    p = jnp.exp2(s - m_new)
                    lrun_ref[m] = alpha * lrun_ref[m] + jnp.sum(p, axis=-1, keepdims=True)
                    acc_ref[sub, m] = alpha * acc_ref[sub, m] + jnp.dot(
                        p.astype(BF16), vv, preferred_element_type=F32)
                    m_ref[m] = m_new
                return c

            lax.fori_loop(lo, hi, online_step, 0)
            for m in range(2):
                ls_ref[sub, m] = jnp.broadcast_to(lrun_ref[m], (t, LANES))
            return carry

        lax.fori_loop(0, n_sub, online_tile, 0)
        finalize()


def _attention_bounds(stats, slopes, bsz, lp):
    stats = stats[:, :, 0].reshape(bsz, -1, N_STAT_ROWS)
    norms = jnp.sqrt(jnp.max(stats[:, :, :4 * N_HEADS], axis=1)).reshape(bsz, N_HEADS, 2, 2)
    lowest_self = jnp.min(stats[:, :, 4 * N_HEADS:], axis=1).reshape(bsz, N_HEADS, 2)
    product = norms[:, :, 0] * norms[:, :, 1] * BOUND_SLACK
    spread = jnp.max(product * BOUND_SLACK - lowest_self, axis=-1) + 1.0
    reach = (SKIP_LOG2_THRESHOLD + spread) / (slopes * LOG2_E)[None, :]
    reach = jnp.ceil(jnp.minimum(reach, float(lp))).astype(jnp.int32) + 1
    return product.reshape(-1), reach.reshape(-1)


def _attention_layer(q, k, v, slopes, bound, reach, lq1, lk1, lq2, lk2, subg, lambda_init):
    bsz, n_heads, _, s, dh = q.shape
    lp = k.shape[3]
    tq, n_sub = SEQ_TILE, QUERY_TILES_PER_STEP
    rows = tq * n_sub
    assert s % rows == 0
    dv = v.shape[-1]
    kv_bytes = 2 * (2 * lp * dh * 2 + lp * dv * 2)
    scratch = (N_BIAS_CLASSES * tq * tq + n_sub * (4 * tq * LANES + 2 * tq * dv)) * 4
    blocks = kv_bytes + 2 * 2 * rows * dh * 2 + 2 * rows * dv * 2 + scratch
    vec = lambda b, h, i, *_: (0, 0)
    grid_spec = pltpu.PrefetchScalarGridSpec(
        num_scalar_prefetch=3,
        grid=(bsz, n_heads, s // rows),
        in_specs=[
            pl.BlockSpec((1, 1, 2, rows, dh), lambda b, h, i, *_: (b, h, 0, i, 0)),
            pl.BlockSpec((1, 1, 2, lp, dh), lambda b, h, i, *_: (b, h, 0, 0, 0)),
            pl.BlockSpec((1, 1, lp, dv), lambda b, h, i, *_: (b, h, 0, 0)),
            pl.BlockSpec((1, dh), vec), pl.BlockSpec((1, dh), vec),
            pl.BlockSpec((1, dh), vec), pl.BlockSpec((1, dh), vec),
            pl.BlockSpec((1, dv), vec),
        ],
        out_specs=pl.BlockSpec((1, rows, dv), lambda b, h, i, *_: (b, i, h)),
        scratch_shapes=[
            pltpu.VMEM((N_BIAS_CLASSES, tq, tq), F32),
            pltpu.VMEM((n_sub, 2, tq, LANES), F32),
            pltpu.VMEM((n_sub, 2, tq, dv), F32),
            pltpu.VMEM((n_sub, 2, tq, LANES), F32),
            pltpu.VMEM((2, tq, 1), F32),
            pltpu.VMEM((2, tq, 1), F32),
        ],
    )
    return pl.pallas_call(
        functools.partial(_attn_kernel, lambda_init=lambda_init),
        out_shape=jax.ShapeDtypeStruct((bsz, s, n_heads * dv), BF16),
        grid_spec=grid_spec,
        compiler_params=pltpu.CompilerParams(
            dimension_semantics=("arbitrary", "arbitrary", "arbitrary"),
            vmem_limit_bytes=_vmem_limit(blocks)),
        name="diff_attention",
    )(slopes, bound, reach, q, k, v, lq1.reshape(1, dh), lk1.reshape(1, dh),
      lq2.reshape(1, dh), lk2.reshape(1, dh), subg.reshape(1, dv))


def _oproj_kernel(h_ref, a_ref, w_ref, out_ref):
    out_ref[...] = h_ref[...] + jnp.dot(a_ref[...], w_ref[...], preferred_element_type=F32)


def _oproj_layer(h, attn, w_bf16, h_tile_of):
    n, d = attn.shape
    tm = TOKEN_TILE
    blocks = 4 * tm * d * 4 + 2 * tm * d * 2 + 2 * d * d * 2 + tm * d * 4
    return pl.pallas_call(
        _oproj_kernel,
        out_shape=jax.ShapeDtypeStruct((n, d), F32),
        grid=(n // tm,),
        in_specs=[
            pl.BlockSpec((tm, d), lambda i: (h_tile_of(i), 0)),
            pl.BlockSpec((tm, d), lambda i: (i, 0)),
            pl.BlockSpec((d, d), lambda i: (0, 0)),
        ],
        out_specs=pl.BlockSpec((tm, d), lambda i: (i, 0)),
        compiler_params=pltpu.CompilerParams(
            dimension_semantics=("parallel",),
            vmem_limit_bytes=_vmem_limit(blocks)),
        name="attn_out_proj",
    )(h, attn, w_bf16)


def _encode(x, p):
    bsz, s, d = x.shape
    assert s % SEQ_TILE == 0 and SEQ_TILE == TOKEN_TILE
    lp = s + SEQ_TILE
    assert (bsz * lp) % WIDE_TOKEN_TILE == 0 and (bsz * s) % WIDE_TOKEN_TILE == 0
    tiles_in, tiles_out = lp // TOKEN_TILE, s // TOKEN_TILE
    front = jnp.concatenate([jnp.zeros((FRONT_PAD, d), x.dtype), p["meta_tokens"]], axis=0)

    h = _pool_layer(x, front, p["mixer_norm_g"][0], p["pool_w"][0], p["pool_scale"][0])
    h = h.reshape(bsz * lp, d)
    h = _ffn_layer(h, p["ffn_norm_g"][0], p["w_gate"], p["w_up"], p["w_down"],
                   p["final_norm_g"], layer=0, final_norm=False)

    lambda_init = 0.8 - 0.6 * math.exp(-0.3 * 1)
    slopes = 2.0 ** (-8.0 * (jnp.arange(N_HEADS, dtype=F32) + 1.0) / N_HEADS)
    q, k, v, sq = _qkv_layer(h, p["mixer_norm_g"][1], p["w_qkv"][0], bsz, lp)
    bound, reach = _attention_bounds(sq, slopes, bsz, lp)
    o = _attention_layer(q, k, v, slopes, bound, reach, p["lambda_q1"][0], p["lambda_k1"][0],
                         p["lambda_q2"][0], p["lambda_k2"][0], p["subln_g"][0], lambda_init)
    h = _oproj_layer(h, o.reshape(bsz * s, d), p["w_o"][0],
                     h_tile_of=lambda i: (i // tiles_out) * tiles_in + i % tiles_out + 1)
    y = _ffn_layer(h, p["ffn_norm_g"][1], p["w_gate"], p["w_up"], p["w_down"],
                   p["final_norm_g"], layer=1, final_norm=True)
    return y.reshape(bsz, s, d)


def kernel(x_prompt, x_sample, meta_tokens, mixer_norm_g, pool_w, pool_scale, w_qkv, lambda_q1,
           lambda_k1, lambda_q2, lambda_k2, subln_g, w_o, ffn_norm_g, w_gate, w_up, w_down,
           final_norm_g):
    p = dict(
        meta_tokens=meta_tokens, mixer_norm_g=mixer_norm_g, pool_scale=pool_scale,
        lambda_q1=lambda_q1, lambda_k1=lambda_k1, lambda_q2=lambda_q2, lambda_k2=lambda_k2,
        subln_g=subln_g, ffn_norm_g=ffn_norm_g, final_norm_g=final_norm_g,
        pool_w=pool_w.astype(BF16), w_qkv=w_qkv.astype(BF16), w_o=w_o.astype(BF16),
        w_gate=w_gate.astype(BF16), w_up=w_up.astype(BF16), w_down=w_down.astype(BF16),
    )
    return (_encode(x_prompt, p), _encode(x_sample, p))
```

```python
import functools
import math

import jax
import jax.numpy as jnp
from jax import lax
from jax.experimental import pallas as pl
from jax.experimental.pallas import tpu as pltpu

D_MODEL = 2048
N_META = 16
N_POOL_GROUPS = 4
POOL_WINDOWS = (2, 4, 8, 16)
POOL_GROUP_DIM = D_MODEL // N_POOL_GROUPS
N_HEADS = 8
HEAD_DIM = 128
V_HEAD_DIM = 2 * HEAD_DIM
RMS_EPS = 1e-6

F32 = jnp.float32
BF16 = jnp.bfloat16

LANES = 128
V7X_VMEM_BYTES = 64 * 1024 * 1024
V7X_VMEM_COMPILER_RESERVE = 8 * 1024 * 1024
V7X_VMEM_KEPT_FREE = 4 * 1024 * 1024

SEQ_TILE = 512
FRONT_PAD = SEQ_TILE - N_META
TOKEN_TILE = 512
WIDE_TOKEN_TILE = 1024
FF_TILE = 512
POOL_HALO = 8
WIDEST_KEY_BLOCK = 16
QUERY_TILES_PER_STEP = 4

LOG2_E = math.log2(math.e)
QK_LOG2_SCALE = HEAD_DIM ** -0.5 * LOG2_E
SKIP_LOG2_THRESHOLD = 90.0 * LOG2_E
MASKED_LOGIT = -1e30
MIN_BOUNDED_ROW_SUM = 2.0 ** -60
BOUND_SLACK = 1.0 + 2.0 ** -6


def _vmem_limit(block_bytes):
    return min(V7X_VMEM_BYTES - V7X_VMEM_KEPT_FREE,
               int(block_bytes) + V7X_VMEM_COMPILER_RESERVE)


def _rmsnorm(x, g):
    ms = jnp.mean(x * x, axis=-1, keepdims=True)
    return x * lax.rsqrt(ms + RMS_EPS) * g


def _pool_kernel(cur_ref, prev_ref, next_ref, front_ref, g_ref, w_ref, scale_ref, out_ref, *,
                 n_pos):
    i = pl.program_id(1)
    n_tiles = pl.num_programs(1)
    tile = out_ref.shape[1]
    x = jnp.where(i == 0, front_ref[...], cur_ref[0])
    front_tail = front_ref[tile - POOL_HALO:, :]
    halo_lo = jnp.where(i >= 2, prev_ref[0], jnp.where(i == 1, front_tail, 0.0))
    halo_hi = jnp.where(i < n_tiles - 1, next_ref[0], 0.0)
    ext = jnp.concatenate([halo_lo, x, halo_hi], axis=0)
    hn = _rmsnorm(ext, g_ref[...])
    n_ext = ext.shape[0]

    def shifted(a, d):
        return pltpu.roll(a, (-d) % n_ext, axis=0)

    pos = i * tile - FRONT_PAD + lax.broadcasted_iota(jnp.int32, (tile, 1), 0)
    valid = pos >= 0
    for gi, w in enumerate(POOL_WINDOWS):
        cols = slice(gi * POOL_GROUP_DIM, (gi + 1) * POOL_GROUP_DIM)
        e = hn[:, cols]
        f = e
        span = 1
        while 2 * span < w:
            f = f + shifted(f, span)
            span *= 2
        if span == POOL_HALO:
            win = f[:tile] + f[POOL_HALO:POOL_HALO + tile]
        else:
            win = (shifted(f, -span) + f)[POOL_HALO:POOL_HALO + tile]
        cnt = jnp.minimum(pos + w // 2, n_pos) - jnp.maximum(pos - w // 2, 0)
        cnt = jnp.maximum(cnt, 1).astype(F32)
        pooled = win / cnt - e[POOL_HALO:POOL_HALO + tile]
        y = jnp.dot(pooled.astype(BF16), w_ref[gi], preferred_element_type=F32)
        y = y * scale_ref[:, cols]
        out_ref[0, :, cols] = jnp.where(valid, x[:, cols] + y, 0.0)


def _pool_layer(x, front, g, w_bf16, scale):
    bsz, s, d = x.shape
    tile = SEQ_TILE
    nt = s // tile + 1
    hb = tile // POOL_HALO
    n_halo_blocks = s // POOL_HALO
    blk = tile * d * 4
    blocks = 6 * blk + 2 * w_bf16.size * 2 + 4 * blk
    return pl.pallas_call(
        functools.partial(_pool_kernel, n_pos=s + N_META),
        out_shape=jax.ShapeDtypeStruct((bsz, nt * tile, d), F32),
        grid=(bsz, nt),
        in_specs=[
            pl.BlockSpec((1, tile, d), lambda b, i: (b, jnp.maximum(i - 1, 0), 0)),
            pl.BlockSpec((1, POOL_HALO, d),
                         lambda b, i: (b, jnp.maximum((i - 1) * hb - 1, 0), 0)),
            pl.BlockSpec((1, POOL_HALO, d),
                         lambda b, i: (b, jnp.minimum(i * hb, n_halo_blocks - 1), 0)),
            pl.BlockSpec((tile, d), lambda b, i: (0, 0)),
            pl.BlockSpec((1, d), lambda b, i: (0, 0)),
            pl.BlockSpec((N_POOL_GROUPS, POOL_GROUP_DIM, POOL_GROUP_DIM), lambda b, i: (0, 0, 0)),
            pl.BlockSpec((1, d), lambda b, i: (0, 0)),
        ],
        out_specs=pl.BlockSpec((1, tile, d), lambda b, i: (b, i, 0)),
        compiler_params=pltpu.CompilerParams(
            dimension_semantics=("parallel", "arbitrary"),
            vmem_limit_bytes=_vmem_limit(blocks)),
        name="pool_mixer",
    )(x, x, x, front, g.reshape(1, d), w_bf16, scale.reshape(1, d))


def _ffn_kernel(x_ref, g_ref, wg_hbm, wu_hbm, wd_hbm, gf_ref, o_ref, hn_ref, wg_buf, wu_buf,
                wd_buf, sem, *, layer, final_norm):
    i = pl.program_id(0)
    tf = wg_buf.shape[2]
    n_chunks = wg_hbm.shape[2] // tf
    first = lax.rem(i * n_chunks, 2)
    other = 1 - first

    def chunk_copies(c, slot):
        col = pl.multiple_of(c * tf, tf)
        return (
            pltpu.make_async_copy(wg_hbm.at[layer, :, pl.ds(col, tf)], wg_buf.at[slot],
                                  sem.at[0, slot]),
            pltpu.make_async_copy(wu_hbm.at[layer, :, pl.ds(col, tf)], wu_buf.at[slot],
                                  sem.at[1, slot]),
            pltpu.make_async_copy(wd_hbm.at[layer, pl.ds(col, tf), :], wd_buf.at[slot],
                                  sem.at[2, slot]),
        )

    def start(c, slot):
        for copy in chunk_copies(c, slot):
            copy.start()

    def wait(c, slot):
        for copy in chunk_copies(c, slot):
            copy.wait()

    def add_chunk(slot, base_ref):
        hn = hn_ref[...]
        gate = jnp.dot(hn, wg_buf[slot], preferred_element_type=F32)
        up = jnp.dot(hn, wu_buf[slot], preferred_element_type=F32)
        act = (gate * jax.nn.sigmoid(gate) * up).astype(BF16)
        for col in range(0, o_ref.shape[1], tf):
            cols = slice(col, col + tf)
            o_ref[:, cols] = base_ref[:, cols] + jnp.dot(act, wd_buf[slot, :, cols],
                                                        preferred_element_type=F32)

    @pl.when(i == 0)
    def _():
        start(0, first)

    hn_ref[...] = _rmsnorm(x_ref[...], g_ref[...]).astype(BF16)
    wait(0, first)
    start(1, other)
    add_chunk(first, x_ref)

    def later_chunk(c, carry):
        slot = lax.rem(first + c, 2)
        following = jnp.where(c + 1 < n_chunks, c + 1, 0)
        wait(c, slot)
        start(following, 1 - slot)
        add_chunk(slot, o_ref)
        return carry

    lax.fori_loop(1, n_chunks, later_chunk, 0)

    @pl.when(i == pl.num_programs(0) - 1)
    def _():
        wait(0, lax.rem(first + n_chunks, 2))

    if final_norm:
        o_ref[...] = _rmsnorm(o_ref[...], gf_ref[...])


def _ffn_layer(x, g, wg, wu, wd, gf, *, layer, final_norm, tm=WIDE_TOKEN_TILE, tf=FF_TILE):
    n, d = x.shape
    blocks = 4 * tm * d * 4 + tm * d * 2 + 2 * 3 * d * tf * 2 + 3 * tm * tf * 4
    hbm = pl.BlockSpec(memory_space=pl.ANY)
    return pl.pallas_call(
        functools.partial(_ffn_kernel, layer=layer, final_norm=final_norm),
        out_shape=jax.ShapeDtypeStruct((n, d), F32),
        grid=(n // tm,),
        in_specs=[
            pl.BlockSpec((tm, d), lambda i: (i, 0)),
            pl.BlockSpec((1, d), lambda i: (0, 0)),
            hbm, hbm, hbm,
            pl.BlockSpec((1, d), lambda i: (0, 0)),
        ],
        out_specs=pl.BlockSpec((tm, d), lambda i: (i, 0)),
        scratch_shapes=[
            pltpu.VMEM((tm, d), BF16),
            pltpu.VMEM((2, d, tf), BF16),
            pltpu.VMEM((2, d, tf), BF16),
            pltpu.VMEM((2, tf, d), BF16),
            pltpu.SemaphoreType.DMA((3, 2)),
        ],
        compiler_params=pltpu.CompilerParams(
            dimension_semantics=("arbitrary",),
            vmem_limit_bytes=_vmem_limit(blocks)),
        name="swiglu_ffn_final" if final_norm else "swiglu_ffn",
    )(x, g.reshape(1, d), wg, wu, wd, gf.reshape(1, d))


N_STAT_ROWS = 6 * N_HEADS


def _qkv_kernel(x_ref, g_ref, w_ref, q_ref, k_ref, v_ref, st_ref):
    hn = _rmsnorm(x_ref[...], g_ref[...]).astype(BF16)

    def put(row, value):
        st_ref[0, row:row + 1, :] = jnp.broadcast_to(value, (1, LANES))

    for h in range(N_HEADS):
        qk = []
        for part, ref in ((0, q_ref), (1, k_ref)):
            c0 = part * D_MODEL + h * V_HEAD_DIM
            r = jnp.dot(hn, w_ref[:, c0:c0 + V_HEAD_DIM], preferred_element_type=F32)
            if part == 0:
                r = r * QK_LOG2_SCALE
            qk.append(r)
            for m in range(2):
                rm = r[:, m * HEAD_DIM:(m + 1) * HEAD_DIM]
                ref[0, h, m] = rm.astype(BF16)
                put((2 * h + part) * 2 + m,
                    jnp.max(jnp.sum(rm * rm, axis=-1, keepdims=True), axis=0, keepdims=True))
        for m in range(2):
            sl = slice(m * HEAD_DIM, (m + 1) * HEAD_DIM)
            self_logit = jnp.sum(qk[0][:, sl] * qk[1][:, sl], axis=-1, keepdims=True)
            put(4 * N_HEADS + 2 * h + m, jnp.min(self_logit, axis=0, keepdims=True))
        c0 = 2 * D_MODEL + h * V_HEAD_DIM
        v_ref[0, h] = jnp.dot(hn, w_ref[:, c0:c0 + V_HEAD_DIM],
                              preferred_element_type=F32).astype(BF16)


def _qkv_layer(x, g, w_bf16, bsz, lp):
    n, d = x.shape
    tm = TOKEN_TILE
    nt = lp // tm
    q_shape = jax.ShapeDtypeStruct((bsz, N_HEADS, 2, lp - tm, HEAD_DIM), BF16)
    k_shape = jax.ShapeDtypeStruct((bsz, N_HEADS, 2, lp, HEAD_DIM), BF16)
    v_shape = jax.ShapeDtypeStruct((bsz, N_HEADS, lp, V_HEAD_DIM), BF16)
    sq_shape = jax.ShapeDtypeStruct((n // tm, N_STAT_ROWS, LANES), F32)
    blocks = 2 * tm * d * 4 + w_bf16.size * 2 + 2 * 3 * tm * d * 2 + tm * d * 2
    return pl.pallas_call(
        _qkv_kernel,
        out_shape=(q_shape, k_shape, v_shape, sq_shape),
        grid=(n // tm,),
        in_specs=[
            pl.BlockSpec((tm, d), lambda i: (i, 0)),
            pl.BlockSpec((1, d), lambda i: (0, 0)),
            pl.BlockSpec((d, 3 * d), lambda i: (0, 0), pipeline_mode=pl.Buffered(1)),
        ],
        out_specs=(
            pl.BlockSpec((1, N_HEADS, 2, tm, HEAD_DIM),
                         lambda i: (i // nt, 0, 0, jnp.maximum(i % nt - 1, 0), 0)),
            pl.BlockSpec((1, N_HEADS, 2, tm, HEAD_DIM), lambda i: (i // nt, 0, 0, i % nt, 0)),
            pl.BlockSpec((1, N_HEADS, tm, V_HEAD_DIM), lambda i: (i // nt, 0, i % nt, 0)),
            pl.BlockSpec((1, N_STAT_ROWS, LANES), lambda i: (i, 0, 0)),
        ),
        compiler_params=pltpu.CompilerParams(
            dimension_semantics=("arbitrary",),
            vmem_limit_bytes=_vmem_limit(blocks)),
        name="qkv_proj",
    )(x, g.reshape(1, d), w_bf16)


BIAS_LEFT, BIAS_DIAG, BIAS_RIGHT, BIAS_FRONT = 0, 1, 2, 3
N_BIAS_CLASSES = 4
_NT_DIMS = (((1,), (1,)), ((), ()))


def _attn_kernel(slope_ref, bound_ref, reach_ref, q_ref, k_ref, v_ref, lq1_ref, lk1_ref, lq2_ref,
                 lk2_ref, subg_ref, o_ref, bias_ref, l_ref, acc_ref, ls_ref, m_ref, lrun_ref, *,
                 lambda_init):
    b, h, step = pl.program_id(0), pl.program_id(1), pl.program_id(2)
    t = bias_ref.shape[1]
    n_sub = q_ref.shape[3] // t
    n_rows = k_ref.shape[3]
    n_chunks = t // LANES
    log2_t = t.bit_length() - 1
    assert t == 1 << log2_t
    sigma = slope_ref[h] * LOG2_E
    bounds = [bound_ref[(b * N_HEADS + h) * 2 + m] for m in range(2)]
    reach = reach_ref[b * N_HEADS + h]

    def key_range(qi):
        lo = lax.shift_right_logical(jnp.maximum(qi * t - reach, 0), log2_t)
        hi = lax.shift_right_logical(jnp.minimum(qi * t + (t - 1) + reach, n_rows - 1), log2_t)
        return lo, hi + 1

    @pl.when(step == 0)
    def _():
        r = lax.broadcasted_iota(jnp.int32, (t, t), 0)
        c = lax.broadcasted_iota(jnp.int32, (t, t), 1)
        left = -sigma * (r - c).astype(F32)
        bias_ref[BIAS_LEFT] = left
        bias_ref[BIAS_DIAG] = -jnp.abs(left)
        bias_ref[BIAS_RIGHT] = -left
        bias_ref[BIAS_FRONT] = jnp.where(c >= FRONT_PAD, left, MASKED_LOGIT)

    def lane_sums(x):
        ones = jnp.ones((x.shape[1], LANES), BF16)
        hi_part = x.astype(BF16)
        lo_part = (x - hi_part.astype(F32)).astype(BF16)
        return (jnp.dot(hi_part, ones, preferred_element_type=F32)
                + jnp.dot(lo_part, ones, preferred_element_type=F32))

    l_ref[...] = jnp.zeros(l_ref.shape, F32)
    acc_ref[...] = jnp.zeros(acc_ref.shape, F32)

    def query_rows(sub, m):
        return q_ref[0, 0, m, pl.ds(pl.multiple_of(sub * t, t), t), :]

    def tile_class(qi, j):
        side = jnp.where(j < qi, BIAS_LEFT, jnp.where(j == qi, BIAS_DIAG, BIAS_RIGHT))
        return jnp.where(j == 0, BIAS_FRONT, side)

    def tile_offset(qi, j):
        gap = jnp.full((1, LANES), jnp.abs(qi - j) * t, jnp.int32).astype(F32)
        return gap * (-sigma)

    def bounded_tile(sub, carry):
        qi = step * n_sub + sub + 1
        lo, hi = key_range(qi)

        def bounded_block(j0, n_blk):
            start = pl.multiple_of(j0 * t, t)
            width = n_blk * t
            classes = [tile_class(qi, j0 + u) for u in range(n_blk)]
            offsets = [tile_offset(qi, j0 + u) for u in range(n_blk)]
            vv = v_ref[0, 0, pl.ds(start, width), :]
            for m in range(2):
                kk = k_ref[0, 0, m, pl.ds(start, width), :]
                s = lax.dot_general(query_rows(sub, m), kk, _NT_DIMS,
                                    preferred_element_type=F32)
                row_sum = None
                ps = []
                for u in range(n_blk):
                    shift = offsets[u] - bounds[m]
                    for cc in range(n_chunks):
                        sl = slice(cc * LANES, (cc + 1) * LANES)
                        su = s[:, u * t + cc * LANES:u * t + (cc + 1) * LANES]
                        p = jnp.exp2(su + bias_ref[classes[u], :, sl] + shift)
                        row_sum = p if row_sum is None else row_sum + p
                        ps.append(p.astype(BF16))
                l_ref[sub, m] += row_sum
                acc_ref[sub, m] += jnp.dot(jnp.concatenate(ps, axis=1), vv,
                                           preferred_element_type=F32)

        n_steps = hi - lo
        odd = jnp.bitwise_and(n_steps, 1)
        partner = jnp.where(jnp.bitwise_and(n_steps, 2) != 0, 2,
                            jnp.where(jnp.bitwise_and(n_steps, 4) != 0, 4,
                                      jnp.where(jnp.bitwise_and(n_steps, 8) != 0, 8, 0)))
        tail = odd * (partner + 1)
        body = n_steps - tail
        log2_wide = WIDEST_KEY_BLOCK.bit_length() - 1
        n_wide = lax.shift_right_logical(body, log2_wide)

        def wide_body(i, c):
            bounded_block(lo + WIDEST_KEY_BLOCK * i, WIDEST_KEY_BLOCK)
            return c

        lax.fori_loop(0, n_wide, wide_body, 0)
        done = n_wide * WIDEST_KEY_BLOCK
        for width in (WIDEST_KEY_BLOCK >> s for s in range(1, log2_wide)):
            covered = done + jnp.bitwise_and(body, (WIDEST_KEY_BLOCK - 1) & ~(2 * width - 1))

            @pl.when(jnp.bitwise_and(body, width) != 0)
            def _(covered=covered, width=width):
                bounded_block(lo + covered, width)
        for width in (1, 3, 5, 9):
            @pl.when(tail == width)
            def _(width=width):
                bounded_block(hi - width, width)
        return carry

    lax.fori_loop(0, n_sub, bounded_tile, 0)

    def finalize():
        lam = (jnp.exp(jnp.sum(lq1_ref[...] * lk1_ref[...], axis=-1, keepdims=True))
               - jnp.exp(jnp.sum(lq2_ref[...] * lk2_ref[...], axis=-1, keepdims=True))
               + lambda_init)
        gain = subg_ref[...] * (1.0 - lambda_init)
        for sub in range(n_sub):
            inv = [1.0 / ls_ref[sub, m] for m in range(2)]
            o = (acc_ref[sub, 0] * jnp.concatenate([inv[0], inv[0]], axis=1)
                 - lam * (acc_ref[sub, 1] * jnp.concatenate([inv[1], inv[1]], axis=1)))
            r = lax.rsqrt(lane_sums(o * o) * (1.0 / V_HEAD_DIM) + RMS_EPS)
            o = o * jnp.concatenate([r, r], axis=1) * gain
            o_ref[0, sub * t:(sub + 1) * t, :] = o.astype(BF16)

    smallest = None
    for sub in range(n_sub):
        for m in range(2):
            ls = lane_sums(l_ref[sub, m])
            ls_ref[sub, m] = ls
            smallest = ls if smallest is None else jnp.minimum(smallest, ls)
    smallest = jnp.min(smallest)
    finalize()

    @pl.when(smallest < MIN_BOUNDED_ROW_SUM)
    def _():
        def online_tile(sub, carry):
            qi = step * n_sub + sub + 1
            lo, hi = key_range(qi)
            m_ref[...] = jnp.full(m_ref.shape, MASKED_LOGIT, F32)
            lrun_ref[...] = jnp.zeros(lrun_ref.shape, F32)
            acc_ref[sub] = jnp.zeros(acc_ref.shape[1:], F32)

            def online_step(j, c):
                start = pl.multiple_of(j * t, t)
                cls = tile_class(qi, j)
                off = tile_offset(qi, j)[:, :1]
                vv = v_ref[0, 0, pl.ds(start, t), :]
                for m in range(2):
                    kk = k_ref[0, 0, m, pl.ds(start, t), :]
                    s = lax.dot_general(query_rows(sub, m), kk, _NT_DIMS,
                                        preferred_element_type=F32)
                    s = s + bias_ref[cls] + off
                    m_prev = m_ref[m]
                    m_new = jnp.maximum(m_prev, jnp.max(s, axis=-1, keepdims=True))
                    alpha = jnp.exp2(m_prev - m_new)
                    p = jnp.exp2(s - m_new)
                    lrun_ref[m] = alpha * lrun_ref[m] + jnp.sum(p, axis=-1, keepdims=True)
                    acc_ref[sub, m] = alpha * acc_ref[sub, m] + jnp.dot(
                        p.astype(BF16), vv, preferred_element_type=F32)
                    m_ref[m] = m_new
                return c

            lax.fori_loop(lo, hi, online_step, 0)
            for m in range(2):
                ls_ref[sub, m] = jnp.broadcast_to(lrun_ref[m], (t, LANES))
            return carry

        lax.fori_loop(0, n_sub, online_tile, 0)
        finalize()


def _attention_bounds(stats, slopes, bsz, lp):
    stats = stats[:, :, 0].reshape(bsz, -1, N_STAT_ROWS)
    norms = jnp.sqrt(jnp.max(stats[:, :, :4 * N_HEADS], axis=1)).reshape(bsz, N_HEADS, 2, 2)
    lowest_self = jnp.min(stats[:, :, 4 * N_HEADS:], axis=1).reshape(bsz, N_HEADS, 2)
    product = norms[:, :, 0] * norms[:, :, 1] * BOUND_SLACK
    spread = jnp.max(product * BOUND_SLACK - lowest_self, axis=-1) + 1.0
    reach = (SKIP_LOG2_THRESHOLD + spread) / (slopes * LOG2_E)[None, :]
    reach = jnp.ceil(jnp.minimum(reach, float(lp))).astype(jnp.int32) + 1
    return product.reshape(-1), reach.reshape(-1)


def _attention_layer(q, k, v, slopes, bound, reach, lq1, lk1, lq2, lk2, subg, lambda_init):
    bsz, n_heads, _, s, dh = q.shape
    lp = k.shape[3]
    tq, n_sub = SEQ_TILE, QUERY_TILES_PER_STEP
    rows = tq * n_sub
    assert s % rows == 0
    dv = v.shape[-1]
    kv_bytes = 2 * (2 * lp * dh * 2 + lp * dv * 2)
    scratch = (N_BIAS_CLASSES * tq * tq + n_sub * (4 * tq * LANES + 2 * tq * dv)) * 4
    blocks = kv_bytes + 2 * 2 * rows * dh * 2 + 2 * rows * dv * 2 + scratch
    vec = lambda b, h, i, *_: (0, 0)
    grid_spec = pltpu.PrefetchScalarGridSpec(
        num_scalar_prefetch=3,
        grid=(bsz, n_heads, s // rows),
        in_specs=[
            pl.BlockSpec((1, 1, 2, rows, dh), lambda b, h, i, *_: (b, h, 0, i, 0)),
            pl.BlockSpec((1, 1, 2, lp, dh), lambda b, h, i, *_: (b, h, 0, 0, 0)),
            pl.BlockSpec((1, 1, lp, dv), lambda b, h, i, *_: (b, h, 0, 0)),
            pl.BlockSpec((1, dh), vec), pl.BlockSpec((1, dh), vec),
            pl.BlockSpec((1, dh), vec), pl.BlockSpec((1, dh), vec),
            pl.BlockSpec((1, dv), vec),
        ],
        out_specs=pl.BlockSpec((1, rows, dv), lambda b, h, i, *_: (b, i, h)),
        scratch_shapes=[
            pltpu.VMEM((N_BIAS_CLASSES, tq, tq), F32),
            pltpu.VMEM((n_sub, 2, tq, LANES), F32),
            pltpu.VMEM((n_sub, 2, tq, dv), F32),
            pltpu.VMEM((n_sub, 2, tq, LANES), F32),
            pltpu.VMEM((2, tq, 1), F32),
            pltpu.VMEM((2, tq, 1), F32),
        ],
    )
    return pl.pallas_call(
        functools.partial(_attn_kernel, lambda_init=lambda_init),
        out_shape=jax.ShapeDtypeStruct((bsz, s, n_heads * dv), BF16),
        grid_spec=grid_spec,
        compiler_params=pltpu.CompilerParams(
            dimension_semantics=("arbitrary", "arbitrary", "arbitrary"),
            vmem_limit_bytes=_vmem_limit(blocks)),
        name="diff_attention",
    )(slopes, bound, reach, q, k, v, lq1.reshape(1, dh), lk1.reshape(1, dh),
      lq2.reshape(1, dh), lk2.reshape(1, dh), subg.reshape(1, dv))


def _oproj_kernel(h_ref, a_ref, w_ref, out_ref):
    out_ref[...] = h_ref[...] + jnp.dot(a_ref[...], w_ref[...], preferred_element_type=F32)


def _oproj_layer(h, attn, w_bf16, h_tile_of):
    n, d = attn.shape
    tm = TOKEN_TILE
    blocks = 4 * tm * d * 4 + 2 * tm * d * 2 + 2 * d * d * 2 + tm * d * 4
    return pl.pallas_call(
        _oproj_kernel,
        out_shape=jax.ShapeDtypeStruct((n, d), F32),
        grid=(n // tm,),
        in_specs=[
            pl.BlockSpec((tm, d), lambda i: (h_tile_of(i), 0)),
            pl.BlockSpec((tm, d), lambda i: (i, 0)),
            pl.BlockSpec((d, d), lambda i: (0, 0)),
        ],
        out_specs=pl.BlockSpec((tm, d), lambda i: (i, 0)),
        compiler_params=pltpu.CompilerParams(
            dimension_semantics=("parallel",),
            vmem_limit_bytes=_vmem_limit(blocks)),
        name="attn_out_proj",
    )(h, attn, w_bf16)


def _encode(x, p):
    bsz, s, d = x.shape
    assert s % SEQ_TILE == 0 and SEQ_TILE == TOKEN_TILE
    lp = s + SEQ_TILE
    assert (bsz * lp) % WIDE_TOKEN_TILE == 0 and (bsz * s) % WIDE_TOKEN_TILE == 0
    tiles_in, tiles_out = lp // TOKEN_TILE, s // TOKEN_TILE
    front = jnp.concatenate([jnp.zeros((FRONT_PAD, d), x.dtype), p["meta_tokens"]], axis=0)

    h = _pool_layer(x, front, p["mixer_norm_g"][0], p["pool_w"][0], p["pool_scale"][0])
    h = h.reshape(bsz * lp, d)
    h = _ffn_layer(h, p["ffn_norm_g"][0], p["w_gate"], p["w_up"], p["w_down"],
                   p["final_norm_g"], layer=0, final_norm=False)

    lambda_init = 0.8 - 0.6 * math.exp(-0.3 * 1)
    slopes = 2.0 ** (-8.0 * (jnp.arange(N_HEADS, dtype=F32) + 1.0) / N_HEADS)
    q, k, v, sq = _qkv_layer(h, p["mixer_norm_g"][1], p["w_qkv"][0], bsz, lp)
    bound, reach = _attention_bounds(sq, slopes, bsz, lp)
    o = _attention_layer(q, k, v, slopes, bound, reach, p["lambda_q1"][0], p["lambda_k1"][0],
                         p["lambda_q2"][0], p["lambda_k2"][0], p["subln_g"][0], lambda_init)
    h = _oproj_layer(h, o.reshape(bsz * s, d), p["w_o"][0],
                     h_tile_of=lambda i: (i // tiles_out) * tiles_in + i % tiles_out + 1)
    y = _ffn_layer(h, p["ffn_norm_g"][1], p["w_gate"], p["w_up"], p["w_down"],
                   p["final_norm_g"], layer=1, final_norm=True)
    return y.reshape(bsz, s, d)


def kernel(x_prompt, x_sample, meta_tokens, mixer_norm_g, pool_w, pool_scale, w_qkv, lambda_q1,
           lambda_k1, lambda_q2, lambda_k2, subln_g, w_o, ffn_norm_g, w_gate, w_up, w_down,
           final_norm_g):
    p = dict(
        meta_tokens=meta_tokens, mixer_norm_g=mixer_norm_g, pool_scale=pool_scale,
        lambda_q1=lambda_q1, lambda_k1=lambda_k1, lambda_q2=lambda_q2, lambda_k2=lambda_k2,
        subln_g=subln_g, ffn_norm_g=ffn_norm_g, final_norm_g=final_norm_g,
        pool_w=pool_w.astype(BF16), w_qkv=w_qkv.astype(BF16), w_o=w_o.astype(BF16),
        w_gate=w_gate.astype(BF16), w_up=w_up.astype(BF16), w_down=w_down.astype(BF16),
    )
    return (_encode(x_prompt, p), _encode(x_sample, p))
```

```python
import functools
import math

import jax
import jax.numpy as jnp
from jax import lax
from jax.experimental import pallas as pl
from jax.experimental.pallas import tpu as pltpu

D_MODEL = 2048
N_META = 16
N_POOL_GROUPS = 4
POOL_WINDOWS = (2, 4, 8, 16)
POOL_GROUP_DIM = D_MODEL // N_POOL_GROUPS
N_HEADS = 8
HEAD_DIM = 128
V_HEAD_DIM = 2 * HEAD_DIM
RMS_EPS = 1e-6

F32 = jnp.float32
BF16 = jnp.bfloat16

LANES = 128
V7X_VMEM_BYTES = 64 * 1024 * 1024
V7X_VMEM_COMPILER_RESERVE = 8 * 1024 * 1024
V7X_VMEM_KEPT_FREE = 4 * 1024 * 1024

SEQ_TILE = 512
FRONT_PAD = SEQ_TILE - N_META
TOKEN_TILE = 512
WIDE_TOKEN_TILE = 1024
FF_TILE = 512
POOL_HALO = 8
WIDEST_KEY_BLOCK = 16
QUERY_TILES_PER_STEP = 4

LOG2_E = math.log2(math.e)
QK_LOG2_SCALE = HEAD_DIM ** -0.5 * LOG2_E
SKIP_LOG2_THRESHOLD = 90.0 * LOG2_E
MASKED_LOGIT = -1e30
MIN_BOUNDED_ROW_SUM = 2.0 ** -60
BOUND_SLACK = 1.0 + 2.0 ** -6


def _vmem_limit(block_bytes):
    return min(V7X_VMEM_BYTES - V7X_VMEM_KEPT_FREE,
               int(block_bytes) + V7X_VMEM_COMPILER_RESERVE)


def _rmsnorm(x, g):
    ms = jnp.mean(x * x, axis=-1, keepdims=True)
    return x * lax.rsqrt(ms + RMS_EPS) * g


def _pool_kernel(cur_ref, prev_ref, next_ref, front_ref, g_ref, w_ref, scale_ref, out_ref, *,
                 n_pos):
    i = pl.program_id(1)
    n_tiles = pl.num_programs(1)
    tile = out_ref.shape[1]
    x = jnp.where(i == 0, front_ref[...], cur_ref[0])
    front_tail = front_ref[tile - POOL_HALO:, :]
    halo_lo = jnp.where(i >= 2, prev_ref[0], jnp.where(i == 1, front_tail, 0.0))
    halo_hi = jnp.where(i < n_tiles - 1, next_ref[0], 0.0)
    ext = jnp.concatenate([halo_lo, x, halo_hi], axis=0)
    hn = _rmsnorm(ext, g_ref[...])
    n_ext = ext.shape[0]

    def shifted(a, d):
        return pltpu.roll(a, (-d) % n_ext, axis=0)

    pos = i * tile - FRONT_PAD + lax.broadcasted_iota(jnp.int32, (tile, 1), 0)
    valid = pos >= 0
    for gi, w in enumerate(POOL_WINDOWS):
        cols = slice(gi * POOL_GROUP_DIM, (gi + 1) * POOL_GROUP_DIM)
        e = hn[:, cols]
        f = e
        span = 1
        while 2 * span < w:
            f = f + shifted(f, span)
            span *= 2
        if span == POOL_HALO:
            win = f[:tile] + f[POOL_HALO:POOL_HALO + tile]
        else:
            win = (shifted(f, -span) + f)[POOL_HALO:POOL_HALO + tile]
        cnt = jnp.minimum(pos + w // 2, n_pos) - jnp.maximum(pos - w // 2, 0)
        cnt = jnp.maximum(cnt, 1).astype(F32)
        pooled = win / cnt - e[POOL_HALO:POOL_HALO + tile]
        y = jnp.dot(pooled.astype(BF16), w_ref[gi], preferred_element_type=F32)
        y = y * scale_ref[:, cols]
        out_ref[0, :, cols] = jnp.where(valid, x[:, cols] + y, 0.0)


def _pool_layer(x, front, g, w_bf16, scale):
    bsz, s, d = x.shape
    tile = SEQ_TILE
    nt = s // tile + 1
    hb = tile // POOL_HALO
    n_halo_blocks = s // POOL_HALO
    blk = tile * d * 4
    blocks = 6 * blk + 2 * w_bf16.size * 2 + 4 * blk
    return pl.pallas_call(
        functools.partial(_pool_kernel, n_pos=s + N_META),
        out_shape=jax.ShapeDtypeStruct((bsz, nt * tile, d), F32),
        grid=(bsz, nt),
        in_specs=[
            pl.BlockSpec((1, tile, d), lambda b, i: (b, jnp.maximum(i - 1, 0), 0)),
            pl.BlockSpec((1, POOL_HALO, d),
                         lambda b, i: (b, jnp.maximum((i - 1) * hb - 1, 0), 0)),
            pl.BlockSpec((1, POOL_HALO, d),
                         lambda b, i: (b, jnp.minimum(i * hb, n_halo_blocks - 1), 0)),
            pl.BlockSpec((tile, d), lambda b, i: (0, 0)),
            pl.BlockSpec((1, d), lambda b, i: (0, 0)),
            pl.BlockSpec((N_POOL_GROUPS, POOL_GROUP_DIM, POOL_GROUP_DIM), lambda b, i: (0, 0, 0)),
            pl.BlockSpec((1, d), lambda b, i: (0, 0)),
        ],
        out_specs=pl.BlockSpec((1, tile, d), lambda b, i: (b, i, 0)),
        compiler_params=pltpu.CompilerParams(
            dimension_semantics=("parallel", "arbitrary"),
            vmem_limit_bytes=_vmem_limit(blocks)),
        name="pool_mixer",
    )(x, x, x, front, g.reshape(1, d), w_bf16, scale.reshape(1, d))


def _ffn_kernel(x_ref, g_ref, wg_hbm, wu_hbm, wd_hbm, gf_ref, o_ref, hn_ref, wg_buf, wu_buf,
                wd_buf, sem, *, layer, final_norm):
    i = pl.program_id(0)
    tf = wg_buf.shape[2]
    n_chunks = wg_hbm.shape[2] // tf
    first = lax.rem(i * n_chunks, 2)
    other = 1 - first

    def chunk_copies(c, slot):
        col = pl.multiple_of(c * tf, tf)
        return (
            pltpu.make_async_copy(wg_hbm.at[layer, :, pl.ds(col, tf)], wg_buf.at[slot],
                                  sem.at[0, slot]),
            pltpu.make_async_copy(wu_hbm.at[layer, :, pl.ds(col, tf)], wu_buf.at[slot],
                                  sem.at[1, slot]),
            pltpu.make_async_copy(wd_hbm.at[layer, pl.ds(col, tf), :], wd_buf.at[slot],
                                  sem.at[2, slot]),
        )

    def start(c, slot):
        for copy in chunk_copies(c, slot):
            copy.start()

    def wait(c, slot):
        for copy in chunk_copies(c, slot):
            copy.wait()

    def add_chunk(slot, base_ref):
        hn = hn_ref[...]
        gate = jnp.dot(hn, wg_buf[slot], preferred_element_type=F32)
        up = jnp.dot(hn, wu_buf[slot], preferred_element_type=F32)
        act = (gate * jax.nn.sigmoid(gate) * up).astype(BF16)
        for col in range(0, o_ref.shape[1], tf):
            cols = slice(col, col + tf)
            o_ref[:, cols] = base_ref[:, cols] + jnp.dot(act, wd_buf[slot, :, cols],
                                                        preferred_element_type=F32)

    @pl.when(i == 0)
    def _():
        start(0, first)

    wait(0, first)
    start(1, other)
    hn_ref[...] = _rmsnorm(x_ref[...], g_ref[...]).astype(BF16)
    add_chunk(first, x_ref)

    def later_chunk(c, carry):
        slot = lax.rem(first + c, 2)
        following = jnp.where(c + 1 < n_chunks, c + 1, 0)
        wait(c, slot)
        start(following, 1 - slot)
        add_chunk(slot, o_ref)
        return carry

    lax.fori_loop(1, n_chunks, later_chunk, 0)

    @pl.when(i == pl.num_programs(0) - 1)
    def _():
        wait(0, lax.rem(first + n_chunks, 2))

    if final_norm:
        o_ref[...] = _rmsnorm(o_ref[...], gf_ref[...])


def _ffn_layer(x, g, wg, wu, wd, gf, *, layer, final_norm, tm=WIDE_TOKEN_TILE, tf=FF_TILE):
    n, d = x.shape
    blocks = 4 * tm * d * 4 + tm * d * 2 + 2 * 3 * d * tf * 2 + 3 * tm * tf * 4
    hbm = pl.BlockSpec(memory_space=pl.ANY)
    return pl.pallas_call(
        functools.partial(_ffn_kernel, layer=layer, final_norm=final_norm),
        out_shape=jax.ShapeDtypeStruct((n, d), F32),
        grid=(n // tm,),
        in_specs=[
            pl.BlockSpec((tm, d), lambda i: (i, 0)),
            pl.BlockSpec((1, d), lambda i: (0, 0)),
            hbm, hbm, hbm,
            pl.BlockSpec((1, d), lambda i: (0, 0)),
        ],
        out_specs=pl.BlockSpec((tm, d), lambda i: (i, 0)),
        scratch_shapes=[
            pltpu.VMEM((tm, d), BF16),
            pltpu.VMEM((2, d, tf), BF16),
            pltpu.VMEM((2, d, tf), BF16),
            pltpu.VMEM((2, tf, d), BF16),
            pltpu.SemaphoreType.DMA((3, 2)),
        ],
        compiler_params=pltpu.CompilerParams(
            dimension_semantics=("arbitrary",),
            vmem_limit_bytes=_vmem_limit(blocks)),
        name="swiglu_ffn_final" if final_norm else "swiglu_ffn",
    )(x, g.reshape(1, d), wg, wu, wd, gf.reshape(1, d))


N_STAT_ROWS = 6 * N_HEADS


def _qkv_kernel(x_ref, g_ref, w_ref, q_ref, k_ref, v_ref, st_ref):
    hn = _rmsnorm(x_ref[...], g_ref[...]).astype(BF16)

    def put(row, value):
        st_ref[0, row:row + 1, :] = jnp.broadcast_to(value, (1, LANES))

    for h in range(N_HEADS):
        qk = []
        for part, ref in ((0, q_ref), (1, k_ref)):
            c0 = part * D_MODEL + h * V_HEAD_DIM
            r = jnp.dot(hn, w_ref[:, c0:c0 + V_HEAD_DIM], preferred_element_type=F32)
            if part == 0:
                r = r * QK_LOG2_SCALE
            qk.append(r)
            for m in range(2):
                rm = r[:, m * HEAD_DIM:(m + 1) * HEAD_DIM]
                ref[0, h, m] = rm.astype(BF16)
                put((2 * h + part) * 2 + m,
                    jnp.max(jnp.sum(rm * rm, axis=-1, keepdims=True), axis=0, keepdims=True))
        for m in range(2):
            sl = slice(m * HEAD_DIM, (m + 1) * HEAD_DIM)
            self_logit = jnp.sum(qk[0][:, sl] * qk[1][:, sl], axis=-1, keepdims=True)
            put(4 * N_HEADS + 2 * h + m, jnp.min(self_logit, axis=0, keepdims=True))
        c0 = 2 * D_MODEL + h * V_HEAD_DIM
        v_ref[0, h] = jnp.dot(hn, w_ref[:, c0:c0 + V_HEAD_DIM],
                              preferred_element_type=F32).astype(BF16)


def _qkv_layer(x, g, w_bf16, bsz, lp):
    n, d = x.shape
    tm = TOKEN_TILE
    nt = lp // tm
    q_shape = jax.ShapeDtypeStruct((bsz, N_HEADS, 2, lp - tm, HEAD_DIM), BF16)
    k_shape = jax.ShapeDtypeStruct((bsz, N_HEADS, 2, lp, HEAD_DIM), BF16)
    v_shape = jax.ShapeDtypeStruct((bsz, N_HEADS, lp, V_HEAD_DIM), BF16)
    sq_shape = jax.ShapeDtypeStruct((n // tm, N_STAT_ROWS, LANES), F32)
    blocks = 2 * tm * d * 4 + w_bf16.size * 2 + 2 * 3 * tm * d * 2 + tm * d * 2
    return pl.pallas_call(
        _qkv_kernel,
        out_shape=(q_shape, k_shape, v_shape, sq_shape),
        grid=(n // tm,),
        in_specs=[
            pl.BlockSpec((tm, d), lambda i: (i, 0)),
            pl.BlockSpec((1, d), lambda i: (0, 0)),
            pl.BlockSpec((d, 3 * d), lambda i: (0, 0), pipeline_mode=pl.Buffered(1)),
        ],
        out_specs=(
            pl.BlockSpec((1, N_HEADS, 2, tm, HEAD_DIM),
                         lambda i: (i // nt, 0, 0, jnp.maximum(i % nt - 1, 0), 0)),
            pl.BlockSpec((1, N_HEADS, 2, tm, HEAD_DIM), lambda i: (i // nt, 0, 0, i % nt, 0)),
            pl.BlockSpec((1, N_HEADS, tm, V_HEAD_DIM), lambda i: (i // nt, 0, i % nt, 0)),
            pl.BlockSpec((1, N_STAT_ROWS, LANES), lambda i: (i, 0, 0)),
        ),
        compiler_params=pltpu.CompilerParams(
            dimension_semantics=("arbitrary",),
            vmem_limit_bytes=_vmem_limit(blocks)),
        name="qkv_proj",
    )(x, g.reshape(1, d), w_bf16)


BIAS_LEFT, BIAS_DIAG, BIAS_RIGHT, BIAS_FRONT = 0, 1, 2, 3
N_BIAS_CLASSES = 4
_NT_DIMS = (((1,), (1,)), ((), ()))


def _attn_kernel(slope_ref, bound_ref, reach_ref, q_ref, k_ref, v_ref, lq1_ref, lk1_ref, lq2_ref,
                 lk2_ref, subg_ref, o_ref, bias_ref, l_ref, acc_ref, ls_ref, m_ref, lrun_ref, *,
                 lambda_init):
    b, h, step = pl.program_id(0), pl.program_id(1), pl.program_id(2)
    t = bias_ref.shape[1]
    n_sub = q_ref.shape[3] // t
    n_rows = k_ref.shape[3]
    n_chunks = t // LANES
    log2_t = t.bit_length() - 1
    assert t == 1 << log2_t
    sigma = slope_ref[h] * LOG2_E
    bounds = [bound_ref[(b * N_HEADS + h) * 2 + m] for m in range(2)]
    reach = reach_ref[b * N_HEADS + h]

    def key_range(qi):
        lo = lax.shift_right_logical(jnp.maximum(qi * t - reach, 0), log2_t)
        hi = lax.shift_right_logical(jnp.minimum(qi * t + (t - 1) + reach, n_rows - 1), log2_t)
        return lo, hi + 1

    @pl.when(step == 0)
    def _():
        r = lax.broadcasted_iota(jnp.int32, (t, t), 0)
        c = lax.broadcasted_iota(jnp.int32, (t, t), 1)
        left = -sigma * (r - c).astype(F32)
        bias_ref[BIAS_LEFT] = left
        bias_ref[BIAS_DIAG] = -jnp.abs(left)
        bias_ref[BIAS_RIGHT] = -left
        bias_ref[BIAS_FRONT] = jnp.where(c >= FRONT_PAD, left, MASKED_LOGIT)

    def lane_sums(x):
        ones = jnp.ones((x.shape[1], LANES), BF16)
        hi_part = x.astype(BF16)
        lo_part = (x - hi_part.astype(F32)).astype(BF16)
        return (jnp.dot(hi_part, ones, preferred_element_type=F32)
                + jnp.dot(lo_part, ones, preferred_element_type=F32))

    l_ref[...] = jnp.zeros(l_ref.shape, F32)
    acc_ref[...] = jnp.zeros(acc_ref.shape, F32)

    def query_rows(sub, m):
        return q_ref[0, 0, m, pl.ds(pl.multiple_of(sub * t, t), t), :]

    def tile_class(qi, j):
        side = jnp.where(j < qi, BIAS_LEFT, jnp.where(j == qi, BIAS_DIAG, BIAS_RIGHT))
        return jnp.where(j == 0, BIAS_FRONT, side)

    def tile_offset(qi, j):
        gap = jnp.full((1, LANES), jnp.abs(qi - j) * t, jnp.int32).astype(F32)
        return gap * (-sigma)

    def bounded_tile(sub, carry):
        qi = step * n_sub + sub + 1
        lo, hi = key_range(qi)

        def bounded_block(j0, n_blk):
            start = pl.multiple_of(j0 * t, t)
            width = n_blk * t
            classes = [tile_class(qi, j0 + u) for u in range(n_blk)]
            offsets = [tile_offset(qi, j0 + u) for u in range(n_blk)]
            vv = v_ref[0, 0, pl.ds(start, width), :]
            for m in range(2):
                kk = k_ref[0, 0, m, pl.ds(start, width), :]
                s = lax.dot_general(query_rows(sub, m), kk, _NT_DIMS,
                                    preferred_element_type=F32)
                row_sum = None
                ps = []
                for u in range(n_blk):
                    shift = offsets[u] - bounds[m]
                    for cc in range(n_chunks):
                        sl = slice(cc * LANES, (cc + 1) * LANES)
                        su = s[:, u * t + cc * LANES:u * t + (cc + 1) * LANES]
                        p = jnp.exp2(su + bias_ref[classes[u], :, sl] + shift)
                        row_sum = p if row_sum is None else row_sum + p
                        ps.append(p.astype(BF16))
                l_ref[sub, m] += row_sum
                acc_ref[sub, m] += jnp.dot(jnp.concatenate(ps, axis=1), vv,
                                           preferred_element_type=F32)

        n_steps = hi - lo
        odd = jnp.bitwise_and(n_steps, 1)
        partner = jnp.where(jnp.bitwise_and(n_steps, 2) != 0, 2,
                            jnp.where(jnp.bitwise_and(n_steps, 4) != 0, 4,
                                      jnp.where(jnp.bitwise_and(n_steps, 8) != 0, 8, 0)))
        tail = odd * (partner + 1)
        body = n_steps - tail
        log2_wide = WIDEST_KEY_BLOCK.bit_length() - 1
        n_wide = lax.shift_right_logical(body, log2_wide)

        def wide_body(i, c):
            bounded_block(lo + WIDEST_KEY_BLOCK * i, WIDEST_KEY_BLOCK)
            return c

        lax.fori_loop(0, n_wide, wide_body, 0)
        done = n_wide * WIDEST_KEY_BLOCK
        for width in (WIDEST_KEY_BLOCK >> s for s in range(1, log2_wide)):
            covered = done + jnp.bitwise_and(body, (WIDEST_KEY_BLOCK - 1) & ~(2 * width - 1))

            @pl.when(jnp.bitwise_and(body, width) != 0)
            def _(covered=covered, width=width):
                bounded_block(lo + covered, width)
        for width in (1, 3, 5, 9):
            @pl.when(tail == width)
            def _(width=width):
                bounded_block(hi - width, width)
        return carry

    lax.fori_loop(0, n_sub, bounded_tile, 0)

    def finalize():
        lam = (jnp.exp(jnp.sum(lq1_ref[...] * lk1_ref[...], axis=-1, keepdims=True))
               - jnp.exp(jnp.sum(lq2_ref[...] * lk2_ref[...], axis=-1, keepdims=True))
               + lambda_init)
        gain = subg_ref[...] * (1.0 - lambda_init)
        for sub in range(n_sub):
            inv = [1.0 / ls_ref[sub, m] for m in range(2)]
            o = (acc_ref[sub, 0] * jnp.concatenate([inv[0], inv[0]], axis=1)
                 - lam * (acc_ref[sub, 1] * jnp.concatenate([inv[1], inv[1]], axis=1)))
            r = lax.rsqrt(lane_sums(o * o) * (1.0 / V_HEAD_DIM) + RMS_EPS)
            o = o * jnp.concatenate([r, r], axis=1) * gain
            o_ref[0, sub * t:(sub + 1) * t, :] = o.astype(BF16)

    smallest = None
    for sub in range(n_sub):
        for m in range(2):
            ls = lane_sums(l_ref[sub, m])
            ls_ref[sub, m] = ls
            smallest = ls if smallest is None else jnp.minimum(smallest, ls)
    smallest = jnp.min(smallest)
    finalize()

    @pl.when(smallest < MIN_BOUNDED_ROW_SUM)
    def _():
        def online_tile(sub, carry):
            qi = step * n_sub + sub + 1
            lo, hi = key_range(qi)
            m_ref[...] = jnp.full(m_ref.shape, MASKED_LOGIT, F32)
            lrun_ref[...] = jnp.zeros(lrun_ref.shape, F32)
            acc_ref[sub] = jnp.zeros(acc_ref.shape[1:], F32)

            def online_step(j, c):
                start = pl.multiple_of(j * t, t)
                cls = tile_class(qi, j)
                off = tile_offset(qi, j)[:, :1]
                vv = v_ref[0, 0, pl.ds(start, t), :]
                for m in range(2):
                    kk = k_ref[0, 0, m, pl.ds(start, t), :]
                    s = lax.dot_general(query_rows(sub, m), kk, _NT_DIMS,
                                        preferred_element_type=F32)
                    s = s + bias_ref[cls] + off
                    m_prev = m_ref[m]
                    m_new = jnp.maximum(m_prev, jnp.max(s, axis=-1, keepdims=True))
                    alpha = jnp.exp2(m_prev - m_new)
                    p = jnp.exp2(s - m_new)
                    lrun_ref[m] = alpha * lrun_ref[m] + jnp.sum(p, axis=-1, keepdims=True)
                    acc_ref[sub, m] = alpha * acc_ref[sub, m] + jnp.dot(
                        p.astype(BF16), vv, preferred_element_type=F32)
                    m_ref[m] = m_new
                return c

            lax.fori_loop(lo, hi, online_step, 0)
            for m in range(2):
                ls_ref[sub, m] = jnp.broadcast_to(lrun_ref[m], (t, LANES))
            return carry

        lax.fori_loop(0, n_sub, online_tile, 0)
        finalize()


def _attention_bounds(stats, slopes, bsz, lp):
    stats = stats[:, :, 0].reshape(bsz, -1, N_STAT_ROWS)
    norms = jnp.sqrt(jnp.max(stats[:, :, :4 * N_HEADS], axis=1)).reshape(bsz, N_HEADS, 2, 2)
    lowest_self = jnp.min(stats[:, :, 4 * N_HEADS:], axis=1).reshape(bsz, N_HEADS, 2)
    product = norms[:, :, 0] * norms[:, :, 1] * BOUND_SLACK
    spread = jnp.max(product * BOUND_SLACK - lowest_self, axis=-1) + 1.0
    reach = (SKIP_LOG2_THRESHOLD + spread) / (slopes * LOG2_E)[None, :]
    reach = jnp.ceil(jnp.minimum(reach, float(lp))).astype(jnp.int32) + 1
    return product.reshape(-1), reach.reshape(-1)


def _attention_layer(q, k, v, slopes, bound, reach, lq1, lk1, lq2, lk2, subg, lambda_init):
    bsz, n_heads, _, s, dh = q.shape
    lp = k.shape[3]
    tq, n_sub = SEQ_TILE, QUERY_TILES_PER_STEP
    rows = tq * n_sub
    assert s % rows == 0
    dv = v.shape[-1]
    kv_bytes = 2 * (2 * lp * dh * 2 + lp * dv * 2)
    scratch = (N_BIAS_CLASSES * tq * tq + n_sub * (4 * tq * LANES + 2 * tq * dv)) * 4
    blocks = kv_bytes + 2 * 2 * rows * dh * 2 + 2 * rows * dv * 2 + scratch
    vec = lambda b, h, i, *_: (0, 0)
    grid_spec = pltpu.PrefetchScalarGridSpec(
        num_scalar_prefetch=3,
        grid=(bsz, n_heads, s // rows),
        in_specs=[
            pl.BlockSpec((1, 1, 2, rows, dh), lambda b, h, i, *_: (b, h, 0, i, 0)),
            pl.BlockSpec((1, 1, 2, lp, dh), lambda b, h, i, *_: (b, h, 0, 0, 0)),
            pl.BlockSpec((1, 1, lp, dv), lambda b, h, i, *_: (b, h, 0, 0)),
            pl.BlockSpec((1, dh), vec), pl.BlockSpec((1, dh), vec),
            pl.BlockSpec((1, dh), vec), pl.BlockSpec((1, dh), vec),
            pl.BlockSpec((1, dv), vec),
        ],
        out_specs=pl.BlockSpec((1, rows, dv), lambda b, h, i, *_: (b, i, h)),
        scratch_shapes=[
            pltpu.VMEM((N_BIAS_CLASSES, tq, tq), F32),
            pltpu.VMEM((n_sub, 2, tq, LANES), F32),
            pltpu.VMEM((n_sub, 2, tq, dv), F32),
            pltpu.VMEM((n_sub, 2, tq, LANES), F32),
            pltpu.VMEM((2, tq, 1), F32),
            pltpu.VMEM((2, tq, 1), F32),
        ],
    )
    return pl.pallas_call(
        functools.partial(_attn_kernel, lambda_init=lambda_init),
        out_shape=jax.ShapeDtypeStruct((bsz, s, n_heads * dv), BF16),
        grid_spec=grid_spec,
        compiler_params=pltpu.CompilerParams(
            dimension_semantics=("arbitrary", "arbitrary", "arbitrary"),
            vmem_limit_bytes=_vmem_limit(blocks)),
        name="diff_attention",
    )(slopes, bound, reach, q, k, v, lq1.reshape(1, dh), lk1.reshape(1, dh),
      lq2.reshape(1, dh), lk2.reshape(1, dh), subg.reshape(1, dv))


def _oproj_kernel(h_ref, a_ref, w_ref, out_ref):
    out_ref[...] = h_ref[...] + jnp.dot(a_ref[...], w_ref[...], preferred_element_type=F32)


def _oproj_layer(h, attn, w_bf16, h_tile_of):
    n, d = attn.shape
    tm = TOKEN_TILE
    blocks = 4 * tm * d * 4 + 2 * tm * d * 2 + 2 * d * d * 2 + tm * d * 4
    return pl.pallas_call(
        _oproj_kernel,
        out_shape=jax.ShapeDtypeStruct((n, d), F32),
        grid=(n // tm,),
        in_specs=[
            pl.BlockSpec((tm, d), lambda i: (h_tile_of(i), 0)),
            pl.BlockSpec((tm, d), lambda i: (i, 0)),
            pl.BlockSpec((d, d), lambda i: (0, 0)),
        ],
        out_specs=pl.BlockSpec((tm, d), lambda i: (i, 0)),
        compiler_params=pltpu.CompilerParams(
            dimension_semantics=("parallel",),
            vmem_limit_bytes=_vmem_limit(blocks)),
        name="attn_out_proj",
    )(h, attn, w_bf16)


def _encode(x, p):
    bsz, s, d = x.shape
    assert s % SEQ_TILE == 0 and SEQ_TILE == TOKEN_TILE
    lp = s + SEQ_TILE
    assert (bsz * lp) % WIDE_TOKEN_TILE == 0 and (bsz * s) % WIDE_TOKEN_TILE == 0
    tiles_in, tiles_out = lp // TOKEN_TILE, s // TOKEN_TILE
    front = jnp.concatenate([jnp.zeros((FRONT_PAD, d), x.dtype), p["meta_tokens"]], axis=0)

    h = _pool_layer(x, front, p["mixer_norm_g"][0], p["pool_w"][0], p["pool_scale"][0])
    h = h.reshape(bsz * lp, d)
    h = _ffn_layer(h, p["ffn_norm_g"][0], p["w_gate"], p["w_up"], p["w_down"],
                   p["final_norm_g"], layer=0, final_norm=False)

    lambda_init = 0.8 - 0.6 * math.exp(-0.3 * 1)
    slopes = 2.0 ** (-8.0 * (jnp.arange(N_HEADS, dtype=F32) + 1.0) / N_HEADS)
    q, k, v, sq = _qkv_layer(h, p["mixer_norm_g"][1], p["w_qkv"][0], bsz, lp)
    bound, reach = _attention_bounds(sq, slopes, bsz, lp)
    o = _attention_layer(q, k, v, slopes, bound, reach, p["lambda_q1"][0], p["lambda_k1"][0],
                         p["lambda_q2"][0], p["lambda_k2"][0], p["subln_g"][0], lambda_init)
    h = _oproj_layer(h, o.reshape(bsz * s, d), p["w_o"][0],
                     h_tile_of=lambda i: (i // tiles_out) * tiles_in + i % tiles_out + 1)
    y = _ffn_layer(h, p["ffn_norm_g"][1], p["w_gate"], p["w_up"], p["w_down"],
                   p["final_norm_g"], layer=1, final_norm=True)
    return y.reshape(bsz, s, d)


def kernel(x_prompt, x_sample, meta_tokens, mixer_norm_g, pool_w, pool_scale, w_qkv, lambda_q1,
           lambda_k1, lambda_q2, lambda_k2, subln_g, w_o, ffn_norm_g, w_gate, w_up, w_down,
           final_norm_g):
    p = dict(
        meta_tokens=meta_tokens, mixer_norm_g=mixer_norm_g, pool_scale=pool_scale,
        lambda_q1=lambda_q1, lambda_k1=lambda_k1, lambda_q2=lambda_q2, lambda_k2=lambda_k2,
        subln_g=subln_g, ffn_norm_g=ffn_norm_g, final_norm_g=final_norm_g,
        pool_w=pool_w.astype(BF16), w_qkv=w_qkv.astype(BF16), w_o=w_o.astype(BF16),
        w_gate=w_gate.astype(BF16), w_up=w_up.astype(BF16), w_down=w_down.astype(BF16),
    )
    return (_encode(x_prompt, p), _encode(x_sample, p))
```

```python
import functools
import math

import jax
import jax.numpy as jnp
from jax import lax
from jax.experimental import pallas as pl
from jax.experimental.pallas import tpu as pltpu

D_MODEL = 2048
N_META = 16
N_POOL_GROUPS = 4
POOL_WINDOWS = (2, 4, 8, 16)
POOL_GROUP_DIM = D_MODEL // N_POOL_GROUPS
N_HEADS = 8
HEAD_DIM = 128
V_HEAD_DIM = 2 * HEAD_DIM
RMS_EPS = 1e-6

F32 = jnp.float32
BF16 = jnp.bfloat16

LANES = 128
V7X_VMEM_BYTES = 64 * 1024 * 1024
V7X_VMEM_COMPILER_RESERVE = 8 * 1024 * 1024
V7X_VMEM_KEPT_FREE = 4 * 1024 * 1024

SEQ_TILE = 512
FRONT_PAD = SEQ_TILE - N_META
TOKEN_TILE = 512
WIDE_TOKEN_TILE = 1024
FF_TILE = 512
POOL_HALO = 8
WIDEST_KEY_BLOCK = 16
QUERY_TILES_PER_STEP = 4

LOG2_E = math.log2(math.e)
QK_LOG2_SCALE = HEAD_DIM ** -0.5 * LOG2_E
SKIP_LOG2_THRESHOLD = 90.0 * LOG2_E
MASKED_LOGIT = -1e30
MIN_BOUNDED_ROW_SUM = 2.0 ** -60
BOUND_SLACK = 1.0 + 2.0 ** -6


def _vmem_limit(block_bytes):
    return min(V7X_VMEM_BYTES - V7X_VMEM_KEPT_FREE,
               int(block_bytes) + V7X_VMEM_COMPILER_RESERVE)


def _rmsnorm(x, g):
    ms = jnp.mean(x * x, axis=-1, keepdims=True)
    return x * lax.rsqrt(ms + RMS_EPS) * g


def _pool_kernel(cur_ref, prev_ref, next_ref, front_ref, g_ref, w_ref, scale_ref, out_ref, *,
                 n_pos):
    i = pl.program_id(1)
    n_tiles = pl.num_programs(1)
    tile = out_ref.shape[1]
    x = jnp.where(i == 0, front_ref[...], cur_ref[0])
    front_tail = front_ref[tile - POOL_HALO:, :]
    halo_lo = jnp.where(i >= 2, prev_ref[0], jnp.where(i == 1, front_tail, 0.0))
    halo_hi = jnp.where(i < n_tiles - 1, next_ref[0], 0.0)
    ext = jnp.concatenate([halo_lo, x, halo_hi], axis=0)
    hn = _rmsnorm(ext, g_ref[...])
    n_ext = ext.shape[0]

    def shifted(a, d):
        return pltpu.roll(a, (-d) % n_ext, axis=0)

    pos = i * tile - FRONT_PAD + lax.broadcasted_iota(jnp.int32, (tile, 1), 0)
    valid = pos >= 0
    for gi, w in enumerate(POOL_WINDOWS):
        cols = slice(gi * POOL_GROUP_DIM, (gi + 1) * POOL_GROUP_DIM)
        e = hn[:, cols]
        f = e
        span = 1
        while 2 * span < w:
            f = f + shifted(f, span)
            span *= 2
        if span == POOL_HALO:
            win = f[:tile] + f[POOL_HALO:POOL_HALO + tile]
        else:
            win = (shifted(f, -span) + f)[POOL_HALO:POOL_HALO + tile]
        cnt = jnp.minimum(pos + w // 2, n_pos) - jnp.maximum(pos - w // 2, 0)
        cnt = jnp.maximum(cnt, 1).astype(F32)
        pooled = win / cnt - e[POOL_HALO:POOL_HALO + tile]
        y = jnp.dot(pooled.astype(BF16), w_ref[gi], preferred_element_type=F32)
        y = y * scale_ref[:, cols]
        out_ref[0, :, cols] = jnp.where(valid, x[:, cols] + y, 0.0)


def _pool_layer(x, front, g, w_bf16, scale):
    bsz, s, d = x.shape
    tile = SEQ_TILE
    nt = s // tile + 1
    hb = tile // POOL_HALO
    n_halo_blocks = s // POOL_HALO
    blk = tile * d * 4
    blocks = 6 * blk + 2 * w_bf16.size * 2 + 4 * blk
    return pl.pallas_call(
        functools.partial(_pool_kernel, n_pos=s + N_META),
        out_shape=jax.ShapeDtypeStruct((bsz, nt * tile, d), F32),
        grid=(bsz, nt),
        in_specs=[
            pl.BlockSpec((1, tile, d), lambda b, i: (b, jnp.maximum(i - 1, 0), 0)),
            pl.BlockSpec((1, POOL_HALO, d),
                         lambda b, i: (b, jnp.maximum((i - 1) * hb - 1, 0), 0)),
            pl.BlockSpec((1, POOL_HALO, d),
                         lambda b, i: (b, jnp.minimum(i * hb, n_halo_blocks - 1), 0)),
            pl.BlockSpec((tile, d), lambda b, i: (0, 0)),
            pl.BlockSpec((1, d), lambda b, i: (0, 0)),
            pl.BlockSpec((N_POOL_GROUPS, POOL_GROUP_DIM, POOL_GROUP_DIM), lambda b, i: (0, 0, 0)),
            pl.BlockSpec((1, d), lambda b, i: (0, 0)),
        ],
        out_specs=pl.BlockSpec((1, tile, d), lambda b, i: (b, i, 0)),
        compiler_params=pltpu.CompilerParams(
            dimension_semantics=("parallel", "arbitrary"),
            vmem_limit_bytes=_vmem_limit(blocks)),
        name="pool_mixer",
    )(x, x, x, front, g.reshape(1, d), w_bf16, scale.reshape(1, d))


def _ffn_kernel(x_ref, g_ref, wg_hbm, wu_hbm, wd_hbm, gf_ref, o_ref, hn_ref, wg_buf, wu_buf,
                wd_buf, sem, *, layer, final_norm):
    i = pl.program_id(0)
    tf = wg_buf.shape[2]
    n_chunks = wg_hbm.shape[2] // tf
    first = lax.rem(i * n_chunks, 2)
    other = 1 - first

    def chunk_copies(c, slot):
        col = pl.multiple_of(c * tf, tf)
        return (
            pltpu.make_async_copy(wg_hbm.at[layer, :, pl.ds(col, tf)], wg_buf.at[slot],
                                  sem.at[0, slot]),
            pltpu.make_async_copy(wu_hbm.at[layer, :, pl.ds(col, tf)], wu_buf.at[slot],
                                  sem.at[1, slot]),
            pltpu.make_async_copy(wd_hbm.at[layer, pl.ds(col, tf), :], wd_buf.at[slot],
                                  sem.at[2, slot]),
        )

    def start(c, slot):
        for n, copy in enumerate(chunk_copies(c, slot)):
            copy.start(priority=n % 2)

    def wait(c, slot):
        for copy in chunk_copies(c, slot):
            copy.wait()

    def add_chunk(slot, base_ref):
        hn = hn_ref[...]
        gate = jnp.dot(hn, wg_buf[slot], preferred_element_type=F32)
        up = jnp.dot(hn, wu_buf[slot], preferred_element_type=F32)
        act = (gate * jax.nn.sigmoid(gate) * up).astype(BF16)
        for col in range(0, o_ref.shape[1], tf):
            cols = slice(col, col + tf)
            o_ref[:, cols] = base_ref[:, cols] + jnp.dot(act, wd_buf[slot, :, cols],
                                                        preferred_element_type=F32)

    @pl.when(i == 0)
    def _():
        start(0, first)

    wait(0, first)
    start(1, other)
    hn_ref[...] = _rmsnorm(x_ref[...], g_ref[...]).astype(BF16)
    add_chunk(first, x_ref)

    def later_chunk(c, carry):
        slot = lax.rem(first + c, 2)
        following = jnp.where(c + 1 < n_chunks, c + 1, 0)
        wait(c, slot)
        start(following, 1 - slot)
        add_chunk(slot, o_ref)
        return carry

    lax.fori_loop(1, n_chunks, later_chunk, 0)

    @pl.when(i == pl.num_programs(0) - 1)
    def _():
        wait(0, lax.rem(first + n_chunks, 2))

    if final_norm:
        o_ref[...] = _rmsnorm(o_ref[...], gf_ref[...])


def _ffn_layer(x, g, wg, wu, wd, gf, *, layer, final_norm, tm=WIDE_TOKEN_TILE, tf=FF_TILE):
    n, d = x.shape
    blocks = 4 * tm * d * 4 + tm * d * 2 + 2 * 3 * d * tf * 2 + 3 * tm * tf * 4
    hbm = pl.BlockSpec(memory_space=pl.ANY)
    return pl.pallas_call(
        functools.partial(_ffn_kernel, layer=layer, final_norm=final_norm),
        out_shape=jax.ShapeDtypeStruct((n, d), F32),
        grid=(n // tm,),
        in_specs=[
            pl.BlockSpec((tm, d), lambda i: (i, 0)),
            pl.BlockSpec((1, d), lambda i: (0, 0)),
            hbm, hbm, hbm,
            pl.BlockSpec((1, d), lambda i: (0, 0)),
        ],
        out_specs=pl.BlockSpec((tm, d), lambda i: (i, 0)),
        scratch_shapes=[
            pltpu.VMEM((tm, d), BF16),
            pltpu.VMEM((2, d, tf), BF16),
            pltpu.VMEM((2, d, tf), BF16),
            pltpu.VMEM((2, tf, d), BF16),
            pltpu.SemaphoreType.DMA((3, 2)),
        ],
        compiler_params=pltpu.CompilerParams(
            dimension_semantics=("arbitrary",),
            vmem_limit_bytes=_vmem_limit(blocks)),
        name="swiglu_ffn_final" if final_norm else "swiglu_ffn",
    )(x, g.reshape(1, d), wg, wu, wd, gf.reshape(1, d))


N_STAT_ROWS = 6 * N_HEADS


def _qkv_kernel(x_ref, g_ref, w_ref, q_ref, k_ref, v_ref, st_ref):
    hn = _rmsnorm(x_ref[...], g_ref[...]).astype(BF16)

    def put(row, value):
        st_ref[0, row:row + 1, :] = jnp.broadcast_to(value, (1, LANES))

    for h in range(N_HEADS):
        qk = []
        for part, ref in ((0, q_ref), (1, k_ref)):
            c0 = part * D_MODEL + h * V_HEAD_DIM
            r = jnp.dot(hn, w_ref[:, c0:c0 + V_HEAD_DIM], preferred_element_type=F32)
            if part == 0:
                r = r * QK_LOG2_SCALE
            qk.append(r)
            for m in range(2):
                rm = r[:, m * HEAD_DIM:(m + 1) * HEAD_DIM]
                ref[0, h, m] = rm.astype(BF16)
                put((2 * h + part) * 2 + m,
                    jnp.max(jnp.sum(rm * rm, axis=-1, keepdims=True), axis=0, keepdims=True))
        for m in range(2):
            sl = slice(m * HEAD_DIM, (m + 1) * HEAD_DIM)
            self_logit = jnp.sum(qk[0][:, sl] * qk[1][:, sl], axis=-1, keepdims=True)
            put(4 * N_HEADS + 2 * h + m, jnp.min(self_logit, axis=0, keepdims=True))
        c0 = 2 * D_MODEL + h * V_HEAD_DIM
        v_ref[0, h] = jnp.dot(hn, w_ref[:, c0:c0 + V_HEAD_DIM],
                              preferred_element_type=F32).astype(BF16)


def _qkv_layer(x, g, w_bf16, bsz, lp):
    n, d = x.shape
    tm = TOKEN_TILE
    nt = lp // tm
    q_shape = jax.ShapeDtypeStruct((bsz, N_HEADS, 2, lp - tm, HEAD_DIM), BF16)
    k_shape = jax.ShapeDtypeStruct((bsz, N_HEADS, 2, lp, HEAD_DIM), BF16)
    v_shape = jax.ShapeDtypeStruct((bsz, N_HEADS, lp, V_HEAD_DIM), BF16)
    sq_shape = jax.ShapeDtypeStruct((n // tm, N_STAT_ROWS, LANES), F32)
    blocks = 2 * tm * d * 4 + w_bf16.size * 2 + 2 * 3 * tm * d * 2 + tm * d * 2
    return pl.pallas_call(
        _qkv_kernel,
        out_shape=(q_shape, k_shape, v_shape, sq_shape),
        grid=(n // tm,),
        in_specs=[
            pl.BlockSpec((tm, d), lambda i: (i, 0)),
            pl.BlockSpec((1, d), lambda i: (0, 0)),
            pl.BlockSpec((d, 3 * d), lambda i: (0, 0), pipeline_mode=pl.Buffered(1)),
        ],
        out_specs=(
            pl.BlockSpec((1, N_HEADS, 2, tm, HEAD_DIM),
                         lambda i: (i // nt, 0, 0, jnp.maximum(i % nt - 1, 0), 0)),
            pl.BlockSpec((1, N_HEADS, 2, tm, HEAD_DIM), lambda i: (i // nt, 0, 0, i % nt, 0)),
            pl.BlockSpec((1, N_HEADS, tm, V_HEAD_DIM), lambda i: (i // nt, 0, i % nt, 0)),
            pl.BlockSpec((1, N_STAT_ROWS, LANES), lambda i: (i, 0, 0)),
        ),
        compiler_params=pltpu.CompilerParams(
            dimension_semantics=("arbitrary",),
            vmem_limit_bytes=_vmem_limit(blocks)),
        name="qkv_proj",
    )(x, g.reshape(1, d), w_bf16)


BIAS_LEFT, BIAS_DIAG, BIAS_RIGHT, BIAS_FRONT = 0, 1, 2, 3
N_BIAS_CLASSES = 4
_NT_DIMS = (((1,), (1,)), ((), ()))


def _attn_kernel(slope_ref, bound_ref, reach_ref, q_ref, k_ref, v_ref, lq1_ref, lk1_ref, lq2_ref,
                 lk2_ref, subg_ref, o_ref, bias_ref, l_ref, acc_ref, ls_ref, m_ref, lrun_ref, *,
                 lambda_init):
    b, h, step = pl.program_id(0), pl.program_id(1), pl.program_id(2)
    t = bias_ref.shape[1]
    n_sub = q_ref.shape[3] // t
    n_rows = k_ref.shape[3]
    n_chunks = t // LANES
    log2_t = t.bit_length() - 1
    assert t == 1 << log2_t
    sigma = slope_ref[h] * LOG2_E
    bounds = [bound_ref[(b * N_HEADS + h) * 2 + m] for m in range(2)]
    reach = reach_ref[b * N_HEADS + h]

    def key_range(qi):
        lo = lax.shift_right_logical(jnp.maximum(qi * t - reach, 0), log2_t)
        hi = lax.shift_right_logical(jnp.minimum(qi * t + (t - 1) + reach, n_rows - 1), log2_t)
        return lo, hi + 1

    @pl.when(step == 0)
    def _():
        r = lax.broadcasted_iota(jnp.int32, (t, t), 0)
        c = lax.broadcasted_iota(jnp.int32, (t, t), 1)
        left = -sigma * (r - c).astype(F32)
        bias_ref[BIAS_LEFT] = left
        bias_ref[BIAS_DIAG] = -jnp.abs(left)
        bias_ref[BIAS_RIGHT] = -left
        bias_ref[BIAS_FRONT] = jnp.where(c >= FRONT_PAD, left, MASKED_LOGIT)

    def lane_sums(x):
        ones = jnp.ones((x.shape[1], LANES), BF16)
        hi_part = x.astype(BF16)
        lo_part = (x - hi_part.astype(F32)).astype(BF16)
        return (jnp.dot(hi_part, ones, preferred_element_type=F32)
                + jnp.dot(lo_part, ones, preferred_element_type=F32))

    l_ref[...] = jnp.zeros(l_ref.shape, F32)
    acc_ref[...] = jnp.zeros(acc_ref.shape, F32)

    def query_rows(sub, m):
        return q_ref[0, 0, m, pl.ds(pl.multiple_of(sub * t, t), t), :]

    def tile_class(qi, j):
        side = jnp.where(j < qi, BIAS_LEFT, jnp.where(j == qi, BIAS_DIAG, BIAS_RIGHT))
        return jnp.where(j == 0, BIAS_FRONT, side)

    def tile_offset(qi, j):
        gap = jnp.full((1, LANES), jnp.abs(qi - j) * t, jnp.int32).astype(F32)
        return gap * (-sigma)

    def bounded_tile(sub, carry):
        qi = step * n_sub + sub + 1
        lo, hi = key_range(qi)

        def bounded_block(j0, n_blk):
            start = pl.multiple_of(j0 * t, t)
            width = n_blk * t
            classes = [tile_class(qi, j0 + u) for u in range(n_blk)]
            offsets = [tile_offset(qi, j0 + u) for u in range(n_blk)]
            vv = v_ref[0, 0, pl.ds(start, width), :]
            for m in range(2):
                kk = k_ref[0, 0, m, pl.ds(start, width), :]
                s = lax.dot_general(query_rows(sub, m), kk, _NT_DIMS,
                                    preferred_element_type=F32)
                row_sum = None
                ps = []
                for u in range(n_blk):
                    shift = offsets[u] - bounds[m]
                    for cc in range(n_chunks):
                        sl = slice(cc * LANES, (cc + 1) * LANES)
                        su = s[:, u * t + cc * LANES:u * t + (cc + 1) * LANES]
                        p = jnp.exp2(su + bias_ref[classes[u], :, sl] + shift)
                        row_sum = p if row_sum is None else row_sum + p
                        ps.append(p.astype(BF16))
                l_ref[sub, m] += row_sum
                acc_ref[sub, m] += jnp.dot(jnp.concatenate(ps, axis=1), vv,
                                           preferred_element_type=F32)

        n_steps = hi - lo
        odd = jnp.bitwise_and(n_steps, 1)
        partner = jnp.where(jnp.bitwise_and(n_steps, 2) != 0, 2,
                            jnp.where(jnp.bitwise_and(n_steps, 4) != 0, 4,
                                      jnp.where(jnp.bitwise_and(n_steps, 8) != 0, 8, 0)))
        tail = odd * (partner + 1)
        body = n_steps - tail
        log2_wide = WIDEST_KEY_BLOCK.bit_length() - 1
        n_wide = lax.shift_right_logical(body, log2_wide)

        def wide_body(i, c):
            bounded_block(lo + WIDEST_KEY_BLOCK * i, WIDEST_KEY_BLOCK)
            return c

        lax.fori_loop(0, n_wide, wide_body, 0)
        done = n_wide * WIDEST_KEY_BLOCK
        for width in (WIDEST_KEY_BLOCK >> s for s in range(1, log2_wide)):
            covered = done + jnp.bitwise_and(body, (WIDEST_KEY_BLOCK - 1) & ~(2 * width - 1))

            @pl.when(jnp.bitwise_and(body, width) != 0)
            def _(covered=covered, width=width):
                bounded_block(lo + covered, width)
        for width in (1, 3, 5, 9):
            @pl.when(tail == width)
            def _(width=width):
                bounded_block(hi - width, width)
        return carry

    lax.fori_loop(0, n_sub, bounded_tile, 0)

    def finalize():
        lam = (jnp.exp(jnp.sum(lq1_ref[...] * lk1_ref[...], axis=-1, keepdims=True))
               - jnp.exp(jnp.sum(lq2_ref[...] * lk2_ref[...], axis=-1, keepdims=True))
               + lambda_init)
        gain = subg_ref[...] * (1.0 - lambda_init)
        for sub in range(n_sub):
            inv = [1.0 / ls_ref[sub, m] for m in range(2)]
            o = (acc_ref[sub, 0] * jnp.concatenate([inv[0], inv[0]], axis=1)
                 - lam * (acc_ref[sub, 1] * jnp.concatenate([inv[1], inv[1]], axis=1)))
            r = lax.rsqrt(lane_sums(o * o) * (1.0 / V_HEAD_DIM) + RMS_EPS)
            o = o * jnp.concatenate([r, r], axis=1) * gain
            o_ref[0, sub * t:(sub + 1) * t, :] = o.astype(BF16)

    smallest = None
    for sub in range(n_sub):
        for m in range(2):
            ls = lane_sums(l_ref[sub, m])
            ls_ref[sub, m] = ls
            smallest = ls if smallest is None else jnp.minimum(smallest, ls)
    smallest = jnp.min(smallest)
    finalize()

    @pl.when(smallest < MIN_BOUNDED_ROW_SUM)
    def _():
        def online_tile(sub, carry):
            qi = step * n_sub + sub + 1
            lo, hi = key_range(qi)
            m_ref[...] = jnp.full(m_ref.shape, MASKED_LOGIT, F32)
            lrun_ref[...] = jnp.zeros(lrun_ref.shape, F32)
            acc_ref[sub] = jnp.zeros(acc_ref.shape[1:], F32)

            def online_step(j, c):
                start = pl.multiple_of(j * t, t)
                cls = tile_class(qi, j)
                off = tile_offset(qi, j)[:, :1]
                vv = v_ref[0, 0, pl.ds(start, t), :]
                for m in range(2):
                    kk = k_ref[0, 0, m, pl.ds(start, t), :]
                    s = lax.dot_general(query_rows(sub, m), kk, _NT_DIMS,
                                        preferred_element_type=F32)
                    s = s + bias_ref[cls] + off
                    m_prev = m_ref[m]
                    m_new = jnp.maximum(m_prev, jnp.max(s, axis=-1, keepdims=True))
                    alpha = jnp.exp2(m_prev - m_new)
                    p = jnp.exp2(s - m_new)
                    lrun_ref[m] = alpha * lrun_ref[m] + jnp.sum(p, axis=-1, keepdims=True)
                    acc_ref[sub, m] = alpha * acc_ref[sub, m] + jnp.dot(
                        p.astype(BF16), vv, preferred_element_type=F32)
                    m_ref[m] = m_new
                return c

            lax.fori_loop(lo, hi, online_step, 0)
            for m in range(2):
                ls_ref[sub, m] = jnp.broadcast_to(lrun_ref[m], (t, LANES))
            return carry

        lax.fori_loop(0, n_sub, online_tile, 0)
        finalize()


def _attention_bounds(stats, slopes, bsz, lp):
    stats = stats[:, :, 0].reshape(bsz, -1, N_STAT_ROWS)
    norms = jnp.sqrt(jnp.max(stats[:, :, :4 * N_HEADS], axis=1)).reshape(bsz, N_HEADS, 2, 2)
    lowest_self = jnp.min(stats[:, :, 4 * N_HEADS:], axis=1).reshape(bsz, N_HEADS, 2)
    product = norms[:, :, 0] * norms[:, :, 1] * BOUND_SLACK
    spread = jnp.max(product * BOUND_SLACK - lowest_self, axis=-1) + 1.0
    reach = (SKIP_LOG2_THRESHOLD + spread) / (slopes * LOG2_E)[None, :]
    reach = jnp.ceil(jnp.minimum(reach, float(lp))).astype(jnp.int32) + 1
    return product.reshape(-1), reach.reshape(-1)


def _attention_layer(q, k, v, slopes, bound, reach, lq1, lk1, lq2, lk2, subg, lambda_init):
    bsz, n_heads, _, s, dh = q.shape
    lp = k.shape[3]
    tq, n_sub = SEQ_TILE, QUERY_TILES_PER_STEP
    rows = tq * n_sub
    assert s % rows == 0
    dv = v.shape[-1]
    kv_bytes = 2 * (2 * lp * dh * 2 + lp * dv * 2)
    scratch = (N_BIAS_CLASSES * tq * tq + n_sub * (4 * tq * LANES + 2 * tq * dv)) * 4
    blocks = kv_bytes + 2 * 2 * rows * dh * 2 + 2 * rows * dv * 2 + scratch
    vec = lambda b, h, i, *_: (0, 0)
    grid_spec = pltpu.PrefetchScalarGridSpec(
        num_scalar_prefetch=3,
        grid=(bsz, n_heads, s // rows),
        in_specs=[
            pl.BlockSpec((1, 1, 2, rows, dh), lambda b, h, i, *_: (b, h, 0, i, 0)),
            pl.BlockSpec((1, 1, 2, lp, dh), lambda b, h, i, *_: (b, h, 0, 0, 0)),
            pl.BlockSpec((1, 1, lp, dv), lambda b, h, i, *_: (b, h, 0, 0)),
            pl.BlockSpec((1, dh), vec), pl.BlockSpec((1, dh), vec),
            pl.BlockSpec((1, dh), vec), pl.BlockSpec((1, dh), vec),
            pl.BlockSpec((1, dv), vec),
        ],
        out_specs=pl.BlockSpec((1, rows, dv), lambda b, h, i, *_: (b, i, h)),
        scratch_shapes=[
            pltpu.VMEM((N_BIAS_CLASSES, tq, tq), F32),
            pltpu.VMEM((n_sub, 2, tq, LANES), F32),
            pltpu.VMEM((n_sub, 2, tq, dv), F32),
            pltpu.VMEM((n_sub, 2, tq, LANES), F32),
            pltpu.VMEM((2, tq, 1), F32),
            pltpu.VMEM((2, tq, 1), F32),
        ],
    )
    return pl.pallas_call(
        functools.partial(_attn_kernel, lambda_init=lambda_init),
        out_shape=jax.ShapeDtypeStruct((bsz, s, n_heads * dv), BF16),
        grid_spec=grid_spec,
        compiler_params=pltpu.CompilerParams(
            dimension_semantics=("arbitrary", "arbitrary", "arbitrary"),
            vmem_limit_bytes=_vmem_limit(blocks)),
        name="diff_attention",
    )(slopes, bound, reach, q, k, v, lq1.reshape(1, dh), lk1.reshape(1, dh),
      lq2.reshape(1, dh), lk2.reshape(1, dh), subg.reshape(1, dv))


def _oproj_kernel(h_ref, a_ref, w_ref, out_ref):
    out_ref[...] = h_ref[...] + jnp.dot(a_ref[...], w_ref[...], preferred_element_type=F32)


def _oproj_layer(h, attn, w_bf16, h_tile_of):
    n, d = attn.shape
    tm = TOKEN_TILE
    blocks = 4 * tm * d * 4 + 2 * tm * d * 2 + 2 * d * d * 2 + tm * d * 4
    return pl.pallas_call(
        _oproj_kernel,
        out_shape=jax.ShapeDtypeStruct((n, d), F32),
        grid=(n // tm,),
        in_specs=[
            pl.BlockSpec((tm, d), lambda i: (h_tile_of(i), 0)),
            pl.BlockSpec((tm, d), lambda i: (i, 0)),
            pl.BlockSpec((d, d), lambda i: (0, 0)),
        ],
        out_specs=pl.BlockSpec((tm, d), lambda i: (i, 0)),
        compiler_params=pltpu.CompilerParams(
            dimension_semantics=("parallel",),
            vmem_limit_bytes=_vmem_limit(blocks)),
        name="attn_out_proj",
    )(h, attn, w_bf16)


def _encode(x, p):
    bsz, s, d = x.shape
    assert s % SEQ_TILE == 0 and SEQ_TILE == TOKEN_TILE
    lp = s + SEQ_TILE
    assert (bsz * lp) % WIDE_TOKEN_TILE == 0 and (bsz * s) % WIDE_TOKEN_TILE == 0
    tiles_in, tiles_out = lp // TOKEN_TILE, s // TOKEN_TILE
    front = jnp.concatenate([jnp.zeros((FRONT_PAD, d), x.dtype), p["meta_tokens"]], axis=0)

    h = _pool_layer(x, front, p["mixer_norm_g"][0], p["pool_w"][0], p["pool_scale"][0])
    h = h.reshape(bsz * lp, d)
    h = _ffn_layer(h, p["ffn_norm_g"][0], p["w_gate"], p["w_up"], p["w_down"],
                   p["final_norm_g"], layer=0, final_norm=False)

    lambda_init = 0.8 - 0.6 * math.exp(-0.3 * 1)
    slopes = 2.0 ** (-8.0 * (jnp.arange(N_HEADS, dtype=F32) + 1.0) / N_HEADS)
    q, k, v, sq = _qkv_layer(h, p["mixer_norm_g"][1], p["w_qkv"][0], bsz, lp)
    bound, reach = _attention_bounds(sq, slopes, bsz, lp)
    o = _attention_layer(q, k, v, slopes, bound, reach, p["lambda_q1"][0], p["lambda_k1"][0],
                         p["lambda_q2"][0], p["lambda_k2"][0], p["subln_g"][0], lambda_init)
    h = _oproj_layer(h, o.reshape(bsz * s, d), p["w_o"][0],
                     h_tile_of=lambda i: (i // tiles_out) * tiles_in + i % tiles_out + 1)
    y = _ffn_layer(h, p["ffn_norm_g"][1], p["w_gate"], p["w_up"], p["w_down"],
                   p["final_norm_g"], layer=1, final_norm=True)
    return y.reshape(bsz, s, d)


def kernel(x_prompt, x_sample, meta_tokens, mixer_norm_g, pool_w, pool_scale, w_qkv, lambda_q1,
           lambda_k1, lambda_q2, lambda_k2, subln_g, w_o, ffn_norm_g, w_gate, w_up, w_down,
           final_norm_g):
    p = dict(
        meta_tokens=meta_tokens, mixer_norm_g=mixer_norm_g, pool_scale=pool_scale,
        lambda_q1=lambda_q1, lambda_k1=lambda_k1, lambda_q2=lambda_q2, lambda_k2=lambda_k2,
        subln_g=subln_g, ffn_norm_g=ffn_norm_g, final_norm_g=final_norm_g,
        pool_w=pool_w.astype(BF16), w_qkv=w_qkv.astype(BF16), w_o=w_o.astype(BF16),
        w_gate=w_gate.astype(BF16), w_up=w_up.astype(BF16), w_down=w_down.astype(BF16),
    )
    return (_encode(x_prompt, p), _encode(x_sample, p))
```
